```python
import math
import jax, jax.numpy as jnp
from jax import lax
import numpy as np

D_MODEL = 1024
BATCH = 4
SEQ = 4096
DEPTH = 2
DEC_BATCH = 32
DEC_SEQ = 2048
PAST_LEN = 128

N_MIXERS = 2
N_SSM_LAYERS = (DEPTH + 1) // 2
N_ATTN_LAYERS = DEPTH // 2
D_FF = 2816
D_INNER = 2 * D_MODEL
SSM_HEAD_DIM = 64
SSM_HEADS = D_INNER // SSM_HEAD_DIM
SSM_GROUPS = 4
D_STATE = 128
D_CONV = 5
CHUNK = 128
CONV_DIM = D_INNER + 2 * SSM_GROUPS * D_STATE
SSM_IN_DIM = D_INNER + CONV_DIM + 2 * SSM_HEADS
ATTN_HEADS = 8
ATTN_HEAD_DIM = 64
ATTN_QK_DIM = ATTN_HEADS * 2 * ATTN_HEAD_DIM
ATTN_V_DIM = ATTN_HEADS * 2 * ATTN_HEAD_DIM
ATTN_QKV_DIM = 2 * ATTN_QK_DIM + ATTN_V_DIM
Q_BLOCK = 128
N_BUCKETS = 32
MAX_DISTANCE = 128
EPS = 1e-6

kernel_name = 'hybrid_bidir_ssd_diffattn_encoder'


def rms_norm(x, g):
    xf = x.astype(jnp.float32)
    y = xf * lax.rsqrt(jnp.mean(xf * xf, axis=-1, keepdims=True) + EPS)
    return (y * g.astype(jnp.float32)).astype(x.dtype)


def swiglu(x, wg, wu, wd):
    return (jax.nn.silu(x @ wg) * (x @ wu)) @ wd


def centred_depthwise_conv(x, w, b):
    pad = D_CONV // 2
    y = lax.conv_general_dilated(x, w[:, None, :].astype(x.dtype), window_strides=(1,), padding=[(pad, pad)], dimension_numbers=('NWC', 'WIO', 'NWC'), feature_group_count=x.shape[-1])
    return y + b.astype(x.dtype)


def ssd_chunked(x, dt, a, b_mat, c_mat):
    bsz, seqlen, nh, hp = x.shape
    ng, ns = b_mat.shape[-2:]
    nc = seqlen // CHUNK
    hpg = nh // ng
    xc = x.astype(jnp.float32).reshape(bsz, nc, CHUNK, ng, hpg, hp)
    dtc = dt.reshape(bsz, nc, CHUNK, ng, hpg)
    bc = b_mat.astype(jnp.float32).reshape(bsz, nc, CHUNK, ng, ns)
    cc = c_mat.astype(jnp.float32).reshape(bsz, nc, CHUNK, ng, ns)
    da_cs = jnp.cumsum(dtc * a.reshape(ng, hpg), axis=2)
    mask = jnp.tril(jnp.ones((CHUNK, CHUNK), dtype=bool))[:, :, None, None]
    seg = da_cs[:, :, :, None] - da_cs[:, :, None, :]
    cb = jnp.einsum('bcign,bcjgn->bcijg', cc, bc)
    w = cb[..., None] * jnp.exp(jnp.where(mask, seg, -jnp.inf)) * dtc[:, :, None]
    y_diag = jnp.einsum('bcijgr,bcjgrp->bcigrp', w, xc)
    decay_states = jnp.exp(da_cs[:, :, -1:] - da_cs) * dtc
    states = jnp.einsum('bcjgn,bcjgrp->bcgrpn', bc, xc * decay_states[..., None])
    chunk_decay = jnp.exp(da_cs[:, :, -1])

    def step(h, inp):
        dec, st = inp
        return h * dec[..., None, None] + st, h

    h0 = jnp.zeros((bsz, ng, hpg, hp, ns), jnp.float32)
    _, prev = lax.scan(step, h0, (jnp.moveaxis(chunk_decay, 1, 0), jnp.moveaxis(states, 1, 0)))
    prev = jnp.moveaxis(prev, 0, 1)
    y_off = jnp.einsum('bcign,bcgrpn->bcigrp', cc, prev) * jnp.exp(da_cs)[..., None]
    return (y_diag + y_off).reshape(bsz, seqlen, nh, hp)


def mamba_mixer(u, w_in, conv_w, conv_b, dt_bias, a_log, d_skip, norm_g, w_out):
    bsz, seqlen, _ = u.shape
    proj = u @ w_in
    z, xbc, dt_raw = jnp.split(proj, [D_INNER, D_INNER + CONV_DIM], axis=-1)
    xbc = jax.nn.silu(centred_depthwise_conv(xbc, conv_w, conv_b))
    xs, b_mat, c_mat = jnp.split(xbc, [D_INNER, D_INNER + SSM_GROUPS * D_STATE], axis=-1)
    xs = xs.reshape(bsz, seqlen, SSM_HEADS, SSM_HEAD_DIM)
    b_mat = b_mat.reshape(bsz, seqlen, SSM_GROUPS, D_STATE)
    c_mat = c_mat.reshape(bsz, seqlen, SSM_GROUPS, D_STATE)
    dt = jax.nn.softplus(dt_raw.astype(jnp.float32).reshape(bsz, seqlen, 2, SSM_HEADS) + dt_bias.astype(jnp.float32))
    a = -jnp.exp(a_log.astype(jnp.float32))
    y_fwd = ssd_chunked(xs, dt[:, :, 0], a[0], b_mat, c_mat)
    y_bwd = ssd_chunked(xs[:, ::-1], dt[:, ::-1, 1], a[1], b_mat[:, ::-1], c_mat[:, ::-1])[:, ::-1]
    y = y_fwd + y_bwd + xs.astype(jnp.float32) * d_skip.astype(jnp.float32)[:, None]
    y = y.reshape(bsz, seqlen, D_INNER) * jax.nn.silu(z.astype(jnp.float32))
    yg = y.reshape(bsz, seqlen, SSM_GROUPS, D_INNER // SSM_GROUPS)
    yg = yg * lax.rsqrt(jnp.mean(yg * yg, axis=-1, keepdims=True) + EPS)
    y = yg.reshape(bsz, seqlen, D_INNER) * norm_g.astype(jnp.float32)
    return y.astype(u.dtype) @ w_out


def relative_bucket(rel):
    half = N_BUCKETS // 2
    max_exact = half // 2
    ret = jnp.where(rel > 0, half, 0)
    n = jnp.abs(rel)
    nf = jnp.maximum(n, 1).astype(jnp.float32)
    large = max_exact + (jnp.log(nf / max_exact) / math.log(MAX_DISTANCE / max_exact) * (half - max_exact)).astype(jnp.int32)
    large = jnp.minimum(large, half - 1)
    return ret + jnp.where(n < max_exact, n, large)


def diff_attention(u, w_qkv, lam, subln_g, w_out, rel_bias, lambda_init):
    bsz, seqlen, _ = u.shape
    qkv = u @ w_qkv
    q, k, v = jnp.split(qkv, [ATTN_QK_DIM, 2 * ATTN_QK_DIM], axis=-1)
    q = q.reshape(bsz, seqlen, ATTN_HEADS, 2, ATTN_HEAD_DIM)
    k = k.reshape(bsz, seqlen, ATTN_HEADS, 2, ATTN_HEAD_DIM)
    v = v.reshape(bsz, seqlen, ATTN_HEADS, 2 * ATTN_HEAD_DIM)
    lf = lam.astype(jnp.float32)
    lam_full = jnp.exp(jnp.sum(lf[0] * lf[1])) - jnp.exp(jnp.sum(lf[2] * lf[3])) + lambda_init
    scale = ATTN_HEAD_DIM ** -0.5
    nblk = seqlen // Q_BLOCK
    qb = jnp.moveaxis(q.reshape(bsz, nblk, Q_BLOCK, ATTN_HEADS, 2, ATTN_HEAD_DIM), 1, 0)
    key_pos = jnp.arange(seqlen)
    bias_table = rel_bias.astype(jnp.float32).T

    def block(args):
        q_blk, blk_idx = args
        q_pos = blk_idx * Q_BLOCK + jnp.arange(Q_BLOCK)
        bias = bias_table[:, relative_bucket(key_pos[None, :] - q_pos[:, None])]
        logits = jnp.einsum('bqhmd,bkhmd->bhmqk', q_blk, k).astype(jnp.float32) * scale + bias[None, :, None]
        p = jax.nn.softmax(logits, axis=-1)
        attn = p[:, :, 0] - lam_full * p[:, :, 1]
        return jnp.einsum('bhqk,bkhd->bqhd', attn.astype(v.dtype), v)

    o = lax.map(block, (qb, jnp.arange(nblk)))
    o = jnp.moveaxis(o, 0, 1).reshape(bsz, seqlen, ATTN_HEADS, 2 * ATTN_HEAD_DIM)
    o = rms_norm(o, subln_g) * (1.0 - lambda_init)
    return o.reshape(bsz, seqlen, ATTN_V_DIM) @ w_out


def trunk(x, norm_pre, norm_post, ffn_w_gate, ffn_w_up, ffn_w_down, ssm_w_in, ssm_conv_w, ssm_conv_b, ssm_dt_bias, ssm_a_log, ssm_d, ssm_norm, ssm_w_out, attn_w_qkv, attn_lambda, attn_subln, attn_w_out, rel_bias):
    for i in range(DEPTH):
        h = swiglu(rms_norm(x, norm_pre[i, 0]), ffn_w_gate[i, 0], ffn_w_up[i, 0], ffn_w_down[i, 0])
        x = x + 0.5 * rms_norm(h, norm_post[i, 0])
        u = rms_norm(x, norm_pre[i, 1])
        j = i // N_MIXERS
        if i % N_MIXERS == 0:
            m = mamba_mixer(u, ssm_w_in[j], ssm_conv_w[j], ssm_conv_b[j], ssm_dt_bias[j], ssm_a_log[j], ssm_d[j], ssm_norm[j], ssm_w_out[j])
        else:
            lambda_init = 0.8 - 0.6 * math.exp(-0.3 * i)
            m = diff_attention(u, attn_w_qkv[j], attn_lambda[j], attn_subln[j], attn_w_out[j], rel_bias, lambda_init)
        x = x + rms_norm(m, norm_post[i, 1])
        h = swiglu(rms_norm(x, norm_pre[i, 2]), ffn_w_gate[i, 1], ffn_w_up[i, 1], ffn_w_down[i, 1])
        x = x + 0.5 * rms_norm(h, norm_post[i, 2])
    return x


def setup_inputs(seed: int = 0) -> dict:
    key = jax.random.key(seed)
    ks = jax.random.split(key, 20)
    f32 = jnp.float32

    def nrm(k, shape, scale):
        return jax.random.normal(k, shape, f32) * scale

    n_a, n_b = N_SSM_LAYERS, N_ATTN_LAYERS
    dt0 = jnp.exp(jax.random.uniform(ks[10], (n_a, 2, SSM_HEADS), f32, math.log(1e-3), math.log(1e-1)))
    return {
        'x_prompt': nrm(ks[0], (BATCH, SEQ, D_MODEL), 1.0),
        'x_sample': nrm(ks[1], (DEC_BATCH, DEC_SEQ, D_MODEL), 1.0),
        'norm_pre': 1.0 + nrm(ks[2], (DEPTH, 3, D_MODEL), 0.05),
        'norm_post': 1.0 + nrm(ks[3], (DEPTH, 3, D_MODEL), 0.05),
        'ffn_w_gate': nrm(ks[4], (DEPTH, 2, D_MODEL, D_FF), D_MODEL ** -0.5),
        'ffn_w_up': nrm(ks[5], (DEPTH, 2, D_MODEL, D_FF), D_MODEL ** -0.5),
        'ffn_w_down': nrm(ks[6], (DEPTH, 2, D_FF, D_MODEL), D_FF ** -0.5),
        'ssm_w_in': nrm(ks[7], (n_a, D_MODEL, SSM_IN_DIM), D_MODEL ** -0.5),
        'ssm_conv_w': nrm(ks[8], (n_a, D_CONV, CONV_DIM), D_CONV ** -0.5),
        'ssm_conv_b': nrm(ks[9], (n_a, CONV_DIM), 0.02),
        'ssm_dt_bias': dt0 + jnp.log(-jnp.expm1(-dt0)),
        'ssm_a_log': jnp.log(jax.random.uniform(ks[11], (n_a, 2, SSM_HEADS), f32, 1.0, 16.0)),
        'ssm_d': 1.0 + nrm(ks[12], (n_a, SSM_HEADS), 0.1),
        'ssm_norm': 1.0 + nrm(ks[13], (n_a, D_INNER), 0.05),
        'ssm_w_out': nrm(ks[14], (n_a, D_INNER, D_MODEL), D_INNER ** -0.5),
        'attn_w_qkv': nrm(ks[15], (n_b, D_MODEL, ATTN_QKV_DIM), D_MODEL ** -0.5),
        'attn_lambda': nrm(ks[16], (n_b, 4, ATTN_HEAD_DIM), 0.1),
        'attn_subln': 1.0 + nrm(ks[17], (n_b, 2 * ATTN_HEAD_DIM), 0.05),
        'attn_w_out': nrm(ks[18], (n_b, ATTN_V_DIM, D_MODEL), ATTN_V_DIM ** -0.5),
        'rel_bias': nrm(ks[19], (N_BUCKETS, ATTN_HEADS), 0.5),
    }


def reference(x_prompt, x_sample, norm_pre, norm_post, ffn_w_gate, ffn_w_up, ffn_w_down, ssm_w_in, ssm_conv_w, ssm_conv_b, ssm_dt_bias, ssm_a_log, ssm_d, ssm_norm, ssm_w_out, attn_w_qkv, attn_lambda, attn_subln, attn_w_out, rel_bias):
    y_prompt = trunk(x_prompt, norm_pre, norm_post, ffn_w_gate, ffn_w_up, ffn_w_down, ssm_w_in, ssm_conv_w, ssm_conv_b, ssm_dt_bias, ssm_a_log, ssm_d, ssm_norm, ssm_w_out, attn_w_qkv, attn_lambda, attn_subln, attn_w_out, rel_bias)
    y_sample = trunk(x_sample, norm_pre, norm_post, ffn_w_gate, ffn_w_up, ffn_w_down, ssm_w_in, ssm_conv_w, ssm_conv_b, ssm_dt_bias, ssm_a_log, ssm_d, ssm_norm, ssm_w_out, attn_w_qkv, attn_lambda, attn_subln, attn_w_out, rel_bias)
    return (y_prompt, y_sample)
```

```python
import functools
import math

import jax
import jax.numpy as jnp
from jax import lax
from jax.experimental import pallas as pl
from jax.experimental.pallas import tpu as pltpu

F32 = jnp.float32
BF16 = jnp.bfloat16

EPS = 1e-6
D_MODEL = 1024
D_FF = 2816
FF_CHUNK = 256

D_INNER = 2048
SSM_HEAD_DIM = 64
SSM_HEADS = 32
SSM_GROUPS = 4
HEADS_PER_GROUP = SSM_HEADS // SSM_GROUPS
D_STATE = 128
D_CONV = 5
CHUNK = 128
BC_DIM = SSM_GROUPS * D_STATE
CONV_DIM = D_INNER + 2 * BC_DIM
GROUP_WIDTH = D_INNER // SSM_GROUPS

ATTN_HEADS = 8
ATTN_HEAD_DIM = 64
ATTN_WIDTH = 2 * ATTN_HEAD_DIM
ATTN_DIM = ATTN_HEADS * ATTN_WIDTH
N_BUCKETS = 32
MAX_DISTANCE = 128

HALO = 8
VMEM_LIMIT = 56 * 1024 * 1024

NT_DIMS = (((1,), (1,)), ((), ()))
TN_DIMS = (((0,), (0,)), ((), ()))


def _cparams(sem):
    return pltpu.CompilerParams(dimension_semantics=sem, vmem_limit_bytes=VMEM_LIMIT)


def _rms(x, g):
    return x * lax.rsqrt(jnp.mean(x * x, axis=-1, keepdims=True) + EPS) * g


def _silu(x):
    return x * (1.0 / (1.0 + jnp.exp(-x)))


def _const_spec(shape):
    nd = len(shape)
    return pl.BlockSpec(shape, lambda *_: (0,) * nd, pipeline_mode=pl.Buffered(1))


def _token_tile(n, pref):
    t = min(pref, n)
    while n % t:
        t //= 2
    return t


def _ffn_kernel(x_ref, gpre_ref, wg_ref, wu_ref, wd_ref, gpost_ref, o_ref, xn_ref, acc_ref):
    xn_ref[...] = _rms(x_ref[...], gpre_ref[...]).astype(BF16)
    for c in range(D_FF // FF_CHUNK):
        sl = slice(c * FF_CHUNK, (c + 1) * FF_CHUNK)
        xn = xn_ref[...]
        g = jnp.dot(xn, wg_ref[:, sl], preferred_element_type=F32)
        u = jnp.dot(xn, wu_ref[:, sl], preferred_element_type=F32)
        h = (_silu(g) * u).astype(BF16)
        d = jnp.dot(h, wd_ref[sl, :], preferred_element_type=F32)
        if c == 0:
            acc_ref[...] = d
        else:
            acc_ref[...] += d
    o_ref[...] = x_ref[...] + 0.5 * _rms(acc_ref[...], gpost_ref[...])


def _ffn(x2, g_pre, wg, wu, wd, g_post):
    n = x2.shape[0]
    tm = _token_tile(n, 512)
    row = pl.BlockSpec((tm, D_MODEL), lambda i: (i, 0))
    return pl.pallas_call(
        _ffn_kernel,
        grid=(n // tm,),
        in_specs=[row, _const_spec((1, D_MODEL)), _const_spec((D_MODEL, D_FF)), _const_spec((D_MODEL, D_FF)),
                  _const_spec((D_FF, D_MODEL)), _const_spec((1, D_MODEL))],
        out_specs=row,
        out_shape=jax.ShapeDtypeStruct((n, D_MODEL), F32),
        scratch_shapes=[pltpu.VMEM((tm, D_MODEL), BF16), pltpu.VMEM((tm, D_MODEL), F32)],
        compiler_params=_cparams(("parallel",)),
        name="ffn",
    )(x2, g_pre, wg, wu, wd, g_post)


def _ssm_in_kernel(x_ref, g_ref, wz_ref, wx_ref, wdt_ref, wdtt_ref, z_ref, xbc_ref, dt_ref, dtt_ref):
    u = _rms(x_ref[0], g_ref[...]).astype(BF16)
    z_ref[0] = jnp.dot(u, wz_ref[...], preferred_element_type=F32)
    xbc_ref[0] = jnp.dot(u, wx_ref[...], preferred_element_type=F32)
    dt_ref[0] = jnp.dot(u, wdt_ref[...], preferred_element_type=F32)
    dtt_ref[0] = lax.dot_general(wdtt_ref[...], u, NT_DIMS, preferred_element_type=F32)


def _ssm_in(x, g, wz, wx, wdt, wdtt):
    b, s, _ = x.shape
    tm = _token_tile(s, 512)
    ndt = 2 * SSM_HEADS

    def tok(width):
        return pl.BlockSpec((1, tm, width), lambda i, j: (i, j, 0))

    return pl.pallas_call(
        _ssm_in_kernel,
        grid=(b, s // tm),
        in_specs=[tok(D_MODEL), _const_spec((1, D_MODEL)), _const_spec((D_MODEL, D_INNER)),
                  _const_spec((D_MODEL, CONV_DIM)), _const_spec((D_MODEL, ndt)), _const_spec((ndt, D_MODEL))],
        out_specs=[tok(D_INNER), tok(CONV_DIM), tok(ndt), pl.BlockSpec((1, ndt, tm), lambda i, j: (i, 0, j))],
        out_shape=[jax.ShapeDtypeStruct((b, s, D_INNER), F32), jax.ShapeDtypeStruct((b, s, CONV_DIM), F32),
                   jax.ShapeDtypeStruct((b, s, ndt), F32), jax.ShapeDtypeStruct((b, ndt, s), F32)],
        compiler_params=_cparams(("parallel", "parallel")),
        name="ssm_in",
    )(x, g, wz, wx, wdt, wdtt)


def _conv_kernel(prev_ref, main_ref, next_ref, w_ref, b_ref, xs_ref, bm_ref, cm_ref, ext_ref):
    j = pl.program_id(1)
    tc = main_ref.shape[1]
    ext_ref[pl.ds(0, HALO), :] = jnp.where(j > 0, prev_ref[0], 0.0)
    ext_ref[pl.ds(HALO, tc), :] = main_ref[0]
    ext_ref[pl.ds(HALO + tc, HALO), :] = jnp.where(j < pl.num_programs(1) - 1, next_ref[0], 0.0)
    pad = D_CONV // 2
    acc = b_ref[...] + w_ref[0:1, :] * ext_ref[pl.ds(HALO - pad, tc), :]
    for k in range(1, D_CONV):
        acc = acc + w_ref[k:k + 1, :] * ext_ref[pl.ds(HALO - pad + k, tc), :]
    y = _silu(acc)
    xs_ref[0] = y[:, :D_INNER]
    bm_ref[0] = y[:, D_INNER:D_INNER + BC_DIM]
    cm_ref[0] = y[:, D_INNER + BC_DIM:]


def _conv(xbc, w, bias):
    b, s, _ = xbc.shape
    tc = _token_tile(s, 256)
    nb = tc // HALO
    last = s // HALO - 1

    def tok(width):
        return pl.BlockSpec((1, tc, width), lambda i, j: (i, j, 0))

    return pl.pallas_call(
        _conv_kernel,
        grid=(b, s // tc),
        in_specs=[pl.BlockSpec((1, HALO, CONV_DIM), lambda i, j: (i, jnp.maximum(j * nb - 1, 0), 0)),
                  tok(CONV_DIM),
                  pl.BlockSpec((1, HALO, CONV_DIM), lambda i, j: (i, jnp.minimum((j + 1) * nb, last), 0)),
                  _const_spec((D_CONV, CONV_DIM)), _const_spec((1, CONV_DIM))],
        out_specs=[tok(D_INNER), tok(BC_DIM), tok(BC_DIM)],
        out_shape=[jax.ShapeDtypeStruct((b, s, D_INNER), F32), jax.ShapeDtypeStruct((b, s, BC_DIM), F32),
                   jax.ShapeDtypeStruct((b, s, BC_DIM), F32)],
        scratch_shapes=[pltpu.VMEM((tc + 2 * HALO, CONV_DIM), F32)],
        compiler_params=_cparams(("parallel", "parallel")),
        name="ssm_conv",
    )(xbc, xbc, xbc, w, bias)


def _cumsum_rows(x):
    idx = lax.broadcasted_iota(jnp.int32, x.shape, 0)
    sh = 1
    while sh < CHUNK:
        x = x + jnp.where(idx >= sh, pltpu.roll(x, sh, 0), 0.0)
        sh *= 2
    return x


def _cumsum_lanes(x):
    idx = lax.broadcasted_iota(jnp.int32, x.shape, 1)
    sh = 1
    while sh < CHUNK:
        x = x + jnp.where(idx >= sh, pltpu.roll(x, sh, 1), 0.0)
        sh *= 2
    return x


def _expand_heads(v, e):
    hi = v.astype(BF16)
    lo = (v - hi.astype(F32)).astype(BF16)
    return jnp.dot(hi, e, preferred_element_type=F32) + jnp.dot(lo, e, preferred_element_type=F32)


def _ssd_kernel(*refs, reverse, final):
    if final:
        (xs_ref, bm_ref, cm_ref, dt_ref, dtt_ref, brow_ref, bcol_ref, arow_ref, acol_ref, e_ref,
         prev_ref, dskip_ref, y_ref, st_ref) = refs
    else:
        (xs_ref, bm_ref, cm_ref, dt_ref, dtt_ref, brow_ref, bcol_ref, arow_ref, acol_ref, e_ref,
         y_ref, st_ref) = refs
    d = 1 if reverse else 0
    hs = slice(d * SSM_HEADS, (d + 1) * SSM_HEADS)

    @pl.when(pl.program_id(1) == 0)
    def _():
        st_ref[...] = jnp.zeros_like(st_ref)

    dt_c = jax.nn.softplus(dt_ref[0, :, hs] + brow_ref[d:d + 1, :])
    dt_r = jax.nn.softplus(dtt_ref[0, hs, :] + bcol_ref[d])
    da_c = dt_c * arow_ref[d:d + 1, :]
    da_r = dt_r * acol_ref[d]
    cs_c = _cumsum_rows(da_c)
    cs_r = _cumsum_lanes(da_r)
    tot = cs_c[CHUNK - 1:CHUNK, :]
    if reverse:
        a_c = da_c - cs_c
        a_r = da_r - cs_r
        off_scale = jnp.exp(tot + a_c)
        st_w = jnp.exp(-a_c) * dt_c
    else:
        a_c = cs_c
        a_r = cs_r
        off_scale = jnp.exp(cs_c)
        st_w = jnp.exp(tot - cs_c) * dt_c
    e = e_ref[...]
    off_x = _expand_heads(off_scale, e)
    stw_x = _expand_heads(st_w, e)
    dec_x = _expand_heads(jnp.broadcast_to(jnp.exp(tot), (HALO, SSM_HEADS)), e)[0:1, :]

    ii = lax.broadcasted_iota(jnp.int32, (CHUNK, CHUNK), 0)
    jj = lax.broadcasted_iota(jnp.int32, (CHUNK, CHUNK), 1)
    keep = (jj >= ii) if reverse else (ii >= jj)

    for g in range(SSM_GROUPS):
        gs = slice(g * D_STATE, (g + 1) * D_STATE)
        ws = slice(g * GROUP_WIDTH, (g + 1) * GROUP_WIDTH)
        bg = bm_ref[0, :, gs].astype(BF16)
        cg = cm_ref[0, :, gs].astype(BF16)
        cb = lax.dot_general(cg, bg, NT_DIMS, preferred_element_type=F32)
        st = st_ref[:, ws]
        y_off = jnp.dot(cg, st.astype(BF16), preferred_element_type=F32) * off_x[:, ws]
        xs_g = xs_ref[0, :, ws]
        xw = (xs_g * stw_x[:, ws]).astype(BF16)
        st_ref[:, ws] = st * dec_x[:, ws] + lax.dot_general(bg, xw, TN_DIMS, preferred_element_type=F32)
        for r in range(HEADS_PER_GROUP):
            h = g * HEADS_PER_GROUP + r
            cs_ = slice(h * SSM_HEAD_DIM, (h + 1) * SSM_HEAD_DIM)
            seg = a_c[:, h:h + 1] - a_r[h:h + 1, :]
            w = cb * jnp.exp(jnp.where(keep, seg, -jnp.inf)) * dt_r[h:h + 1, :]
            x_h = xs_ref[0, :, cs_]
            y_h = jnp.dot(w.astype(BF16), x_h.astype(BF16), preferred_element_type=F32)
            y_h = y_h + y_off[:, r * SSM_HEAD_DIM:(r + 1) * SSM_HEAD_DIM]
            if final:
                y_h = y_h + prev_ref[0, :, cs_] + x_h * dskip_ref[:, cs_]
            y_ref[0, :, cs_] = y_h


def _ssd(xs, bm, cm, dt, dtt, brow, bcol, arow, acol, e, prev=None, dskip=None, *, reverse):
    b, s, _ = xs.shape
    nc = s // CHUNK
    final = prev is not None
    ndt = 2 * SSM_HEADS

    def cidx(j):
        return nc - 1 - j if reverse else j

    def tok(width):
        return pl.BlockSpec((1, CHUNK, width), lambda i, j: (i, cidx(j), 0))

    in_specs = [tok(D_INNER), tok(BC_DIM), tok(BC_DIM), tok(ndt),
                pl.BlockSpec((1, ndt, CHUNK), lambda i, j: (i, 0, cidx(j))),
                _const_spec((2, SSM_HEADS)), _const_spec((2, SSM_HEADS, 1)),
                _const_spec((2, SSM_HEADS)), _const_spec((2, SSM_HEADS, 1)),
                _const_spec((SSM_HEADS, D_INNER))]
    args = [xs, bm, cm, dt, dtt, brow, bcol, arow, acol, e]
    if final:
        in_specs += [tok(D_INNER), _const_spec((1, D_INNER))]
        args += [prev, dskip]
    return pl.pallas_call(
        functools.partial(_ssd_kernel, reverse=reverse, final=final),
        grid=(b, nc),
        in_specs=in_specs,
        out_specs=tok(D_INNER),
        out_shape=jax.ShapeDtypeStruct((b, s, D_INNER), F32),
        scratch_shapes=[pltpu.VMEM((D_STATE, D_INNER), F32)],
        compiler_params=_cparams(("parallel", "arbitrary")),
        name="ssd_bwd" if reverse else "ssd_fwd",
    )(*args)


def _ssm_out_kernel(x_ref, y_ref, z_ref, ng_ref, w_ref, gpost_ref, o_ref):
    y = y_ref[...] * _silu(z_ref[...])
    parts = []
    for g in range(SSM_GROUPS):
        yg = y[:, g * GROUP_WIDTH:(g + 1) * GROUP_WIDTH]
        parts.append(yg * lax.rsqrt(jnp.mean(yg * yg, axis=-1, keepdims=True) + EPS))
    yn = (jnp.concatenate(parts, axis=-1) * ng_ref[...]).astype(BF16)
    m = jnp.dot(yn, w_ref[...], preferred_element_type=F32)
    o_ref[...] = x_ref[...] + _rms(m, gpost_ref[...])


def _ssm_out(x2, y2, z2, ng, w, g_post):
    n = x2.shape[0]
    tm = _token_tile(n, 512)

    def row(width):
        return pl.BlockSpec((tm, width), lambda i: (i, 0))

    return pl.pallas_call(
        _ssm_out_kernel,
        grid=(n // tm,),
        in_specs=[row(D_MODEL), row(D_INNER), row(D_INNER), _const_spec((1, D_INNER)),
                  _const_spec((D_INNER, D_MODEL)), _const_spec((1, D_MODEL))],
        out_specs=row(D_MODEL),
        out_shape=jax.ShapeDtypeStruct((n, D_MODEL), F32),
        compiler_params=_cparams(("parallel",)),
        name="ssm_out",
    )(x2, y2, z2, ng, w, g_post)


def _qkv_kernel(x_ref, g_ref, w_ref, o_ref):
    u = _rms(x_ref[...], g_ref[...]).astype(BF16)
    o_ref[...] = jnp.dot(u, w_ref[...], preferred_element_type=F32).astype(BF16)


def _qkv(x2, g, w):
    n = x2.shape[0]
    tm = _token_tile(n, 512)
    return pl.pallas_call(
        _qkv_kernel,
        grid=(n // tm,),
        in_specs=[pl.BlockSpec((tm, D_MODEL), lambda i: (i, 0)), _const_spec((1, D_MODEL)),
                  _const_spec((D_MODEL, 3 * ATTN_DIM))],
        out_specs=pl.BlockSpec((tm, 3 * ATTN_DIM), lambda i: (i, 0)),
        out_shape=jax.ShapeDtypeStruct((n, 3 * ATTN_DIM), BF16),
        compiler_params=_cparams(("parallel",)),
        name="attn_qkv",
    )(x2, g, w)


def _attn_kernel(lam_ref, q_ref, k_ref, v_ref, bias_ref, g_ref, o_ref, *, out_scale):
    tq = q_ref.shape[1]
    s = k_ref.shape[1]
    qi = pl.program_id(2)
    q = q_ref[0]
    k = k_ref[0]
    v = v_ref[0]
    lane = lax.broadcasted_iota(jnp.int32, q.shape, 1)
    start = pl.multiple_of(s - tq - qi * tq, 128)
    bias = bias_ref[0, :, pl.ds(start, s)]
    outs = []
    for m in range(2):
        half = (lane >= ATTN_HEAD_DIM) if m else (lane < ATTN_HEAD_DIM)
        qm = jnp.where(half, q, jnp.zeros_like(q))
        logits = lax.dot_general(qm, k, NT_DIMS, preferred_element_type=F32) + bias
        mx = jnp.max(logits, axis=-1, keepdims=True)
        p = jnp.exp(logits - mx)
        l = jnp.sum(p, axis=-1, keepdims=True)
        outs.append(jnp.dot(p.astype(BF16), v, preferred_element_type=F32) / l)
    o = outs[0] - lam_ref[0] * outs[1]
    o_ref[0] = _rms(o, g_ref[...]) * out_scale


def _attention(lam, qkv, bias_ext, subln, *, out_scale):
    b, s, _ = qkv.shape
    tq = bias_ext.shape[1]
    return pl.pallas_call(
        functools.partial(_attn_kernel, out_scale=out_scale),
        grid=(ATTN_HEADS, b, s // tq),
        in_specs=[pl.BlockSpec(memory_space=pltpu.SMEM),
                  pl.BlockSpec((1, tq, ATTN_WIDTH), lambda h, i, j: (i, j, h)),
                  pl.BlockSpec((1, s, ATTN_WIDTH), lambda h, i, j: (i, 0, ATTN_HEADS + h)),
                  pl.BlockSpec((1, s, ATTN_WIDTH), lambda h, i, j: (i, 0, 2 * ATTN_HEADS + h)),
                  pl.BlockSpec((1, tq, 2 * s), lambda h, i, j: (h, 0, 0)),
                  _const_spec((1, ATTN_WIDTH))],
        out_specs=pl.BlockSpec((1, tq, ATTN_WIDTH), lambda h, i, j: (i, j, h)),
        out_shape=jax.ShapeDtypeStruct((b, s, ATTN_DIM), F32),
        compiler_params=_cparams(("parallel", "parallel", "parallel")),
        name="diff_attn",
    )(lam, qkv, qkv, qkv, bias_ext, subln)


def _proj_out_kernel(x_ref, a_ref, w_ref, gpost_ref, o_ref):
    m = jnp.dot(a_ref[...].astype(BF16), w_ref[...], preferred_element_type=F32)
    o_ref[...] = x_ref[...] + _rms(m, gpost_ref[...])


def _proj_out(x2, a2, w, g_post):
    n = x2.shape[0]
    k = a2.shape[1]
    tm = _token_tile(n, 512)
    return pl.pallas_call(
        _proj_out_kernel,
        grid=(n // tm,),
        in_specs=[pl.BlockSpec((tm, D_MODEL), lambda i: (i, 0)), pl.BlockSpec((tm, k), lambda i: (i, 0)),
                  _const_spec((k, D_MODEL)), _const_spec((1, D_MODEL))],
        out_specs=pl.BlockSpec((tm, D_MODEL), lambda i: (i, 0)),
        out_shape=jax.ShapeDtypeStruct((n, D_MODEL), F32),
        compiler_params=_cparams(("parallel",)),
        name="proj_out",
    )(x2, a2, w, g_post)


def _relative_bucket(rel):
    half = N_BUCKETS // 2
    max_exact = half // 2
    ret = jnp.where(rel > 0, half, 0)
    n = jnp.abs(rel)
    nf = jnp.maximum(n, 1).astype(F32)
    large = max_exact + (jnp.log(nf / max_exact) / math.log(MAX_DISTANCE / max_exact) * (half - max_exact)).astype(jnp.int32)
    large = jnp.minimum(large, half - 1)
    return ret + jnp.where(n < max_exact, n, large)


def _bias_ext(rel_bias, s, tq):
    a = jnp.arange(tq)[:, None]
    c = jnp.arange(2 * s)[None, :]
    rel = c - a - (s - tq)
    return rel_bias.astype(F32).T[:, _relative_bucket(rel)]


def _row(v):
    return v.reshape(1, -1).astype(F32)


def _mamba_layer(x, pre_g, post_g, w_in, conv_w, conv_b, dt_bias, a_log, d_skip, norm_g, w_out):
    b, s, _ = x.shape
    wz = w_in[:, :D_INNER].astype(BF16)
    wx = w_in[:, D_INNER:D_INNER + CONV_DIM].astype(BF16)
    wdt = w_in[:, D_INNER + CONV_DIM:].astype(BF16)
    z, xbc, dt, dtt = _ssm_in(x, _row(pre_g), wz, wx, wdt, wdt.T)
    xs, bm, cm = _conv(xbc, conv_w.astype(F32), _row(conv_b))
    a = -jnp.exp(a_log.astype(F32))
    brow = dt_bias.astype(F32)
    e = jnp.repeat(jnp.eye(SSM_HEADS, dtype=BF16), SSM_HEAD_DIM, axis=1)
    dskip = jnp.repeat(d_skip.astype(F32), SSM_HEAD_DIM).reshape(1, D_INNER)
    common = (xs, bm, cm, dt, dtt, brow, brow[:, :, None], a, a[:, :, None], e)
    y_fwd = _ssd(*common, reverse=False)
    y = _ssd(*common, y_fwd, dskip, reverse=True)
    out = _ssm_out(x.reshape(b * s, D_MODEL), y.reshape(b * s, D_INNER), z.reshape(b * s, D_INNER),
                   _row(norm_g), w_out.astype(BF16), _row(post_g))
    return out.reshape(b, s, D_MODEL)


def _attn_layer(x, pre_g, post_g, w_qkv, lam, subln_g, w_out, rel_bias, lambda_init):
    b, s, _ = x.shape
    scale = ATTN_HEAD_DIM ** -0.5
    col_scale = jnp.concatenate([jnp.full((ATTN_DIM,), scale, F32), jnp.ones((2 * ATTN_DIM,), F32)])
    w = (w_qkv.astype(F32) * col_scale).astype(BF16)
    qkv = _qkv(x.reshape(b * s, D_MODEL), _row(pre_g), w).reshape(b, s, 3 * ATTN_DIM)
    lf = lam.astype(F32)
    lam_full = jnp.exp(jnp.sum(lf[0] * lf[1])) - jnp.exp(jnp.sum(lf[2] * lf[3])) + lambda_init
    tq = _token_tile(s, 256)
    o = _attention(lam_full.reshape(1), qkv, _bias_ext(rel_bias, s, tq), _row(subln_g),
                   out_scale=1.0 - lambda_init)
    out = _proj_out(x.reshape(b * s, D_MODEL), o.reshape(b * s, ATTN_DIM), w_out.astype(BF16), _row(post_g))
    return out.reshape(b, s, D_MODEL)


def _trunk(x, norm_pre, norm_post, ffn_w_gate, ffn_w_up, ffn_w_down, ssm_w_in, ssm_conv_w, ssm_conv_b,
           ssm_dt_bias, ssm_a_log, ssm_d, ssm_norm, ssm_w_out, attn_w_qkv, attn_lambda, attn_subln,
           attn_w_out, rel_bias):
    b, s, _ = x.shape
    depth = norm_pre.shape[0]

    def ffn(x, i, k, slot):
        y = _ffn(x.reshape(b * s, D_MODEL), _row(norm_pre[i, slot]), ffn_w_gate[i, k].astype(BF16),
                 ffn_w_up[i, k].astype(BF16), ffn_w_down[i, k].astype(BF16), _row(norm_post[i, slot]))
        return y.reshape(b, s, D_MODEL)

    for i in range(depth):
        x = ffn(x, i, 0, 0)
        j = i // 2
        if i % 2 == 0:
            x = _mamba_layer(x, norm_pre[i, 1], norm_post[i, 1], ssm_w_in[j], ssm_conv_w[j], ssm_conv_b[j],
                             ssm_dt_bias[j], ssm_a_log[j], ssm_d[j], ssm_norm[j], ssm_w_out[j])
        else:
            lambda_init = 0.8 - 0.6 * math.exp(-0.3 * i)
            x = _attn_layer(x, norm_pre[i, 1], norm_post[i, 1], attn_w_qkv[j], attn_lambda[j], attn_subln[j],
                            attn_w_out[j], rel_bias, lambda_init)
        x = ffn(x, i, 1, 2)
    return x


def kernel(x_prompt, x_sample, norm_pre, norm_post, ffn_w_gate, ffn_w_up, ffn_w_down, ssm_w_in, ssm_conv_w, ssm_conv_b, ssm_dt_bias, ssm_a_log, ssm_d, ssm_norm, ssm_w_out, attn_w_qkv, attn_lambda, attn_subln, attn_w_out, rel_bias):
    params = (norm_pre, norm_post, ffn_w_gate, ffn_w_up, ffn_w_down, ssm_w_in, ssm_conv_w, ssm_conv_b,
              ssm_dt_bias, ssm_a_log, ssm_d, ssm_norm, ssm_w_out, attn_w_qkv, attn_lambda, attn_subln,
              attn_w_out, rel_bias)
    return (_trunk(x_prompt, *params), _trunk(x_sample, *params))
```

```python
import functools
import math

import jax
import jax.numpy as jnp
from jax import lax
from jax.experimental import pallas as pl
from jax.experimental.pallas import tpu as pltpu

F32 = jnp.float32
BF16 = jnp.bfloat16

EPS = 1e-6
D_MODEL = 1024
D_FF = 2816
FF_CHUNK = 256

D_INNER = 2048
SSM_HEAD_DIM = 64
SSM_HEADS = 32
SSM_GROUPS = 4
HEADS_PER_GROUP = SSM_HEADS // SSM_GROUPS
D_STATE = 128
D_CONV = 5
CHUNK = 128
BC_DIM = SSM_GROUPS * D_STATE
CONV_DIM = D_INNER + 2 * BC_DIM
GROUP_WIDTH = D_INNER // SSM_GROUPS

ATTN_HEADS = 8
ATTN_HEAD_DIM = 64
ATTN_WIDTH = 2 * ATTN_HEAD_DIM
ATTN_DIM = ATTN_HEADS * ATTN_WIDTH
N_BUCKETS = 32
MAX_DISTANCE = 128
ATTN_Q_TILE = 512
SLAB_REACH = 2
LOG2E = math.log2(math.e)

HALO = 8
VMEM_LIMIT = 56 * 1024 * 1024

NT_DIMS = (((1,), (1,)), ((), ()))
TN_DIMS = (((0,), (0,)), ((), ()))


def _cparams(sem):
    return pltpu.CompilerParams(dimension_semantics=sem, vmem_limit_bytes=VMEM_LIMIT)


def _rms(x, g):
    return x * lax.rsqrt(jnp.mean(x * x, axis=-1, keepdims=True) + EPS) * g


def _silu(x):
    return x * (1.0 / (1.0 + jnp.exp(-x)))


def _const_spec(shape):
    nd = len(shape)
    return pl.BlockSpec(shape, lambda *_: (0,) * nd, pipeline_mode=pl.Buffered(1))


def _token_tile(n, pref):
    t = min(pref, n)
    while n % t:
        t //= 2
    return t


def _ffn_kernel(x_ref, gpre_ref, wg_ref, wu_ref, wd_ref, gpost_ref, o_ref, xn_ref, acc_ref):
    xn_ref[...] = _rms(x_ref[...], gpre_ref[...]).astype(BF16)
    for c in range(D_FF // FF_CHUNK):
        sl = slice(c * FF_CHUNK, (c + 1) * FF_CHUNK)
        xn = xn_ref[...]
        g = jnp.dot(xn, wg_ref[:, sl], preferred_element_type=F32)
        u = jnp.dot(xn, wu_ref[:, sl], preferred_element_type=F32)
        h = (_silu(g) * u).astype(BF16)
        d = jnp.dot(h, wd_ref[sl, :], preferred_element_type=F32)
        if c == 0:
            acc_ref[...] = d
        else:
            acc_ref[...] += d
    o_ref[...] = x_ref[...] + 0.5 * _rms(acc_ref[...], gpost_ref[...])


def _ffn(x2, g_pre, wg, wu, wd, g_post):
    n = x2.shape[0]
    tm = _token_tile(n, 512)
    row = pl.BlockSpec((tm, D_MODEL), lambda i: (i, 0))
    return pl.pallas_call(
        _ffn_kernel,
        grid=(n // tm,),
        in_specs=[row, _const_spec((1, D_MODEL)), _const_spec((D_MODEL, D_FF)), _const_spec((D_MODEL, D_FF)),
                  _const_spec((D_FF, D_MODEL)), _const_spec((1, D_MODEL))],
        out_specs=row,
        out_shape=jax.ShapeDtypeStruct((n, D_MODEL), F32),
        scratch_shapes=[pltpu.VMEM((tm, D_MODEL), BF16), pltpu.VMEM((tm, D_MODEL), F32)],
        compiler_params=_cparams(("parallel",)),
        name="ffn",
    )(x2, g_pre, wg, wu, wd, g_post)


def _ssm_in_kernel(x_ref, g_ref, wz_ref, wx_ref, wdt_ref, wdtt_ref, z_ref, xbc_ref, dt_ref, dtt_ref):
    u = _rms(x_ref[0], g_ref[...]).astype(BF16)
    z_ref[0] = jnp.dot(u, wz_ref[...], preferred_element_type=F32)
    xbc_ref[0] = jnp.dot(u, wx_ref[...], preferred_element_type=F32)
    dt_ref[0] = jnp.dot(u, wdt_ref[...], preferred_element_type=F32)
    dtt_ref[0] = lax.dot_general(wdtt_ref[...], u, NT_DIMS, preferred_element_type=F32)


def _ssm_in(x, g, wz, wx, wdt, wdtt):
    b, s, _ = x.shape
    tm = _token_tile(s, 512)
    ndt = 2 * SSM_HEADS

    def tok(width):
        return pl.BlockSpec((1, tm, width), lambda i, j: (i, j, 0))

    return pl.pallas_call(
        _ssm_in_kernel,
        grid=(b, s // tm),
        in_specs=[tok(D_MODEL), _const_spec((1, D_MODEL)), _const_spec((D_MODEL, D_INNER)),
                  _const_spec((D_MODEL, CONV_DIM)), _const_spec((D_MODEL, ndt)), _const_spec((ndt, D_MODEL))],
        out_specs=[tok(D_INNER), tok(CONV_DIM), tok(ndt), pl.BlockSpec((1, ndt, tm), lambda i, j: (i, 0, j))],
        out_shape=[jax.ShapeDtypeStruct((b, s, D_INNER), F32), jax.ShapeDtypeStruct((b, s, CONV_DIM), F32),
                   jax.ShapeDtypeStruct((b, s, ndt), F32), jax.ShapeDtypeStruct((b, ndt, s), F32)],
        compiler_params=_cparams(("parallel", "parallel")),
        name="ssm_in",
    )(x, g, wz, wx, wdt, wdtt)


def _conv_kernel(prev_ref, main_ref, next_ref, w_ref, b_ref, xs_ref, bm_ref, cm_ref, ext_ref):
    j = pl.program_id(1)
    tc = main_ref.shape[1]
    ext_ref[pl.ds(0, HALO), :] = jnp.where(j > 0, prev_ref[0], 0.0)
    ext_ref[pl.ds(HALO, tc), :] = main_ref[0]
    ext_ref[pl.ds(HALO + tc, HALO), :] = jnp.where(j < pl.num_programs(1) - 1, next_ref[0], 0.0)
    pad = D_CONV // 2
    acc = b_ref[...] + w_ref[0:1, :] * ext_ref[pl.ds(HALO - pad, tc), :]
    for k in range(1, D_CONV):
        acc = acc + w_ref[k:k + 1, :] * ext_ref[pl.ds(HALO - pad + k, tc), :]
    y = _silu(acc)
    xs_ref[0] = y[:, :D_INNER]
    bm_ref[0] = y[:, D_INNER:D_INNER + BC_DIM]
    cm_ref[0] = y[:, D_INNER + BC_DIM:]


def _conv(xbc, w, bias):
    b, s, _ = xbc.shape
    tc = _token_tile(s, 256)
    nb = tc // HALO
    last = s // HALO - 1

    def tok(width):
        return pl.BlockSpec((1, tc, width), lambda i, j: (i, j, 0))

    return pl.pallas_call(
        _conv_kernel,
        grid=(b, s // tc),
        in_specs=[pl.BlockSpec((1, HALO, CONV_DIM), lambda i, j: (i, jnp.maximum(j * nb - 1, 0), 0)),
                  tok(CONV_DIM),
                  pl.BlockSpec((1, HALO, CONV_DIM), lambda i, j: (i, jnp.minimum((j + 1) * nb, last), 0)),
                  _const_spec((D_CONV, CONV_DIM)), _const_spec((1, CONV_DIM))],
        out_specs=[tok(D_INNER), tok(BC_DIM), tok(BC_DIM)],
        out_shape=[jax.ShapeDtypeStruct((b, s, D_INNER), F32), jax.ShapeDtypeStruct((b, s, BC_DIM), F32),
                   jax.ShapeDtypeStruct((b, s, BC_DIM), F32)],
        scratch_shapes=[pltpu.VMEM((tc + 2 * HALO, CONV_DIM), F32)],
        compiler_params=_cparams(("parallel", "parallel")),
        name="ssm_conv",
    )(xbc, xbc, xbc, w, bias)


def _cumsum_rows(x):
    idx = lax.broadcasted_iota(jnp.int32, x.shape, 0)
    sh = 1
    while sh < CHUNK:
        x = x + jnp.where(idx >= sh, pltpu.roll(x, sh, 0), 0.0)
        sh *= 2
    return x


def _cumsum_lanes(x):
    idx = lax.broadcasted_iota(jnp.int32, x.shape, 1)
    sh = 1
    while sh < CHUNK:
        x = x + jnp.where(idx >= sh, pltpu.roll(x, sh, 1), 0.0)
        sh *= 2
    return x


def _expand_heads(v, e):
    hi = v.astype(BF16)
    lo = (v - hi.astype(F32)).astype(BF16)
    return jnp.dot(hi, e, preferred_element_type=F32) + jnp.dot(lo, e, preferred_element_type=F32)


def _ssd_kernel(*refs, reverse, final):
    if final:
        (xs_ref, bm_ref, cm_ref, dt_ref, dtt_ref, brow_ref, bcol_ref, arow_ref, acol_ref, e_ref,
         prev_ref, dskip_ref, y_ref, st_ref) = refs
    else:
        (xs_ref, bm_ref, cm_ref, dt_ref, dtt_ref, brow_ref, bcol_ref, arow_ref, acol_ref, e_ref,
         y_ref, st_ref) = refs
    d = 1 if reverse else 0
    hs = slice(d * SSM_HEADS, (d + 1) * SSM_HEADS)

    @pl.when(pl.program_id(1) == 0)
    def _():
        st_ref[...] = jnp.zeros_like(st_ref)

    dt_c = jax.nn.softplus(dt_ref[0, :, hs] + brow_ref[d:d + 1, :])
    dt_r = jax.nn.softplus(dtt_ref[0, hs, :] + bcol_ref[d])
    da_c = dt_c * arow_ref[d:d + 1, :]
    da_r = dt_r * acol_ref[d]
    cs_c = _cumsum_rows(da_c)
    cs_r = _cumsum_lanes(da_r)
    tot = cs_c[CHUNK - 1:CHUNK, :]
    if reverse:
        a_c = da_c - cs_c
        a_r = da_r - cs_r
        off_scale = jnp.exp(tot + a_c)
        st_w = jnp.exp(-a_c) * dt_c
    else:
        a_c = cs_c
        a_r = cs_r
        off_scale = jnp.exp(cs_c)
        st_w = jnp.exp(tot - cs_c) * dt_c
    e = e_ref[...]
    off_x = _expand_heads(off_scale, e)
    stw_x = _expand_heads(st_w, e)
    dec_x = _expand_heads(jnp.broadcast_to(jnp.exp(tot), (HALO, SSM_HEADS)), e)[0:1, :]

    ii = lax.broadcasted_iota(jnp.int32, (CHUNK, CHUNK), 0)
    jj = lax.broadcasted_iota(jnp.int32, (CHUNK, CHUNK), 1)
    keep = (jj >= ii) if reverse else (ii >= jj)

    for g in range(SSM_GROUPS):
        gs = slice(g * D_STATE, (g + 1) * D_STATE)
        ws = slice(g * GROUP_WIDTH, (g + 1) * GROUP_WIDTH)
        bg = bm_ref[0, :, gs].astype(BF16)
        cg = cm_ref[0, :, gs].astype(BF16)
        cb = lax.dot_general(cg, bg, NT_DIMS, preferred_element_type=F32)
        st = st_ref[:, ws]
        y_off = jnp.dot(cg, st.astype(BF16), preferred_element_type=F32) * off_x[:, ws]
        xs_g = xs_ref[0, :, ws]
        xw = (xs_g * stw_x[:, ws]).astype(BF16)
        st_ref[:, ws] = st * dec_x[:, ws] + lax.dot_general(bg, xw, TN_DIMS, preferred_element_type=F32)
        for r in range(HEADS_PER_GROUP):
            h = g * HEADS_PER_GROUP + r
            cs_ = slice(h * SSM_HEAD_DIM, (h + 1) * SSM_HEAD_DIM)
            seg = a_c[:, h:h + 1] - a_r[h:h + 1, :]
            w = cb * jnp.exp(jnp.where(keep, seg, -jnp.inf)) * dt_r[h:h + 1, :]
            x_h = xs_ref[0, :, cs_]
            y_h = jnp.dot(w.astype(BF16), x_h.astype(BF16), preferred_element_type=F32)
            y_h = y_h + y_off[:, r * SSM_HEAD_DIM:(r + 1) * SSM_HEAD_DIM]
            if final:
                y_h = y_h + prev_ref[0, :, cs_] + x_h * dskip_ref[:, cs_]
            y_ref[0, :, cs_] = y_h


def _ssd(xs, bm, cm, dt, dtt, brow, bcol, arow, acol, e, prev=None, dskip=None, *, reverse):
    b, s, _ = xs.shape
    nc = s // CHUNK
    final = prev is not None
    ndt = 2 * SSM_HEADS

    def cidx(j):
        return nc - 1 - j if reverse else j

    def tok(width):
        return pl.BlockSpec((1, CHUNK, width), lambda i, j: (i, cidx(j), 0))

    in_specs = [tok(D_INNER), tok(BC_DIM), tok(BC_DIM), tok(ndt),
                pl.BlockSpec((1, ndt, CHUNK), lambda i, j: (i, 0, cidx(j))),
                _const_spec((2, SSM_HEADS)), _const_spec((2, SSM_HEADS, 1)),
                _const_spec((2, SSM_HEADS)), _const_spec((2, SSM_HEADS, 1)),
                _const_spec((SSM_HEADS, D_INNER))]
    args = [xs, bm, cm, dt, dtt, brow, bcol, arow, acol, e]
    if final:
        in_specs += [tok(D_INNER), _const_spec((1, D_INNER))]
        args += [prev, dskip]
    return pl.pallas_call(
        functools.partial(_ssd_kernel, reverse=reverse, final=final),
        grid=(b, nc),
        in_specs=in_specs,
        out_specs=tok(D_INNER),
        out_shape=jax.ShapeDtypeStruct((b, s, D_INNER), F32),
        scratch_shapes=[pltpu.VMEM((D_STATE, D_INNER), F32)],
        compiler_params=_cparams(("parallel", "arbitrary")),
        name="ssd_bwd" if reverse else "ssd_fwd",
    )(*args)


def _ssm_out_kernel(x_ref, y_ref, z_ref, ng_ref, w_ref, gpost_ref, o_ref):
    y = y_ref[...] * _silu(z_ref[...])
    parts = []
    for g in range(SSM_GROUPS):
        yg = y[:, g * GROUP_WIDTH:(g + 1) * GROUP_WIDTH]
        parts.append(yg * lax.rsqrt(jnp.mean(yg * yg, axis=-1, keepdims=True) + EPS))
    yn = (jnp.concatenate(parts, axis=-1) * ng_ref[...]).astype(BF16)
    m = jnp.dot(yn, w_ref[...], preferred_element_type=F32)
    o_ref[...] = x_ref[...] + _rms(m, gpost_ref[...])


def _ssm_out(x2, y2, z2, ng, w, g_post):
    n = x2.shape[0]
    tm = _token_tile(n, 512)

    def row(width):
        return pl.BlockSpec((tm, width), lambda i: (i, 0))

    return pl.pallas_call(
        _ssm_out_kernel,
        grid=(n // tm,),
        in_specs=[row(D_MODEL), row(D_INNER), row(D_INNER), _const_spec((1, D_INNER)),
                  _const_spec((D_INNER, D_MODEL)), _const_spec((1, D_MODEL))],
        out_specs=row(D_MODEL),
        out_shape=jax.ShapeDtypeStruct((n, D_MODEL), F32),
        compiler_params=_cparams(("parallel",)),
        name="ssm_out",
    )(x2, y2, z2, ng, w, g_post)


def _qkv_kernel(x_ref, g_ref, w_ref, o_ref):
    u = _rms(x_ref[...], g_ref[...]).astype(BF16)
    o_ref[...] = jnp.dot(u, w_ref[...], preferred_element_type=F32).astype(BF16)


def _qkv(x2, g, w):
    n = x2.shape[0]
    tm = _token_tile(n, 512)
    return pl.pallas_call(
        _qkv_kernel,
        grid=(n // tm,),
        in_specs=[pl.BlockSpec((tm, D_MODEL), lambda i: (i, 0)), _const_spec((1, D_MODEL)),
                  _const_spec((D_MODEL, 3 * ATTN_DIM))],
        out_specs=pl.BlockSpec((tm, 3 * ATTN_DIM), lambda i: (i, 0)),
        out_shape=jax.ShapeDtypeStruct((n, 3 * ATTN_DIM), BF16),
        compiler_params=_cparams(("parallel",)),
        name="attn_qkv",
    )(x2, g, w)


def _attn_kernel(lam_ref, tbl_ref, bucket_ref, q_ref, k_ref, v_ref, g_ref, o_ref, slab_ref, s_ref, p_ref, vv_ref, *,
                 out_scale):
    tq = slab_ref.shape[0]
    s = k_ref.shape[1]
    nk = s // tq
    h = pl.program_id(0)

    @pl.when(pl.program_id(1) == 0)
    def _():
        bk = bucket_ref[...]
        acc = jnp.zeros(bk.shape, F32)
        for kk in range(N_BUCKETS):
            acc = jnp.where(bk == kk, tbl_ref[h, kk], acc)
        slab_ref[...] = acc

    lam = lam_ref[0]
    g = g_ref[...]
    vv_ref[...] = jnp.concatenate([v_ref[0], jnp.ones((s, ATTN_WIDTH), BF16)], axis=1)
    kchunk = 2 * tq if s % (2 * tq) == 0 else tq

    def q_tile(qi, carry):
        q0 = pl.multiple_of(qi * tq, tq)
        q = q_ref[0, pl.ds(q0, tq), :]
        lane = lax.broadcasted_iota(jnp.int32, q.shape, 1)
        zero = jnp.zeros_like(q)
        q2 = jnp.concatenate([jnp.where(lane < ATTN_HEAD_DIM, q, zero), jnp.where(lane >= ATTN_HEAD_DIM, q, zero)],
                             axis=0)
        mrun = jnp.full((2 * tq, 128), -jnp.inf, F32)
        for k0 in range(0, s, kchunk):
            tiles = []
            for kc in range(k0 // tq, (k0 + kchunk) // tq):
                d = jnp.clip(kc - qi, -SLAB_REACH, SLAB_REACH) + SLAB_REACH
                tiles.append(slab_ref[:, pl.ds(pl.multiple_of(d * tq, tq), tq)])
            bias = jnp.concatenate(tiles, axis=1) if len(tiles) > 1 else tiles[0]
            sc = lax.dot_general(q2, k_ref[0, k0:k0 + kchunk, :], NT_DIMS, preferred_element_type=F32)
            sc = sc + jnp.concatenate([bias, bias], axis=0)
            s_ref[:, k0:k0 + kchunk] = sc
            for j in range(kchunk // 128):
                mrun = jnp.maximum(mrun, sc[:, j * 128:(j + 1) * 128])
        mx = jnp.max(mrun, axis=-1, keepdims=True)
        p_ref[...] = jnp.exp2(s_ref[...] - mx).astype(BF16)
        vv = vv_ref[...]
        acc1 = jnp.dot(p_ref[:tq, :], vv, preferred_element_type=F32)
        acc2 = jnp.dot(p_ref[tq:, :], vv, preferred_element_type=F32)
        o1 = acc1[:, :ATTN_WIDTH] / acc1[:, ATTN_WIDTH:ATTN_WIDTH + 1]
        o2 = acc2[:, :ATTN_WIDTH] / acc2[:, ATTN_WIDTH:ATTN_WIDTH + 1]
        o_ref[0, pl.ds(q0, tq), :] = _rms(o1 - lam * o2, g) * out_scale
        return carry

    lax.fori_loop(0, s // tq, q_tile, 0, unroll=2)


def _attention(lam, tbl, bucket, qkv, subln, *, out_scale):
    b, s, _ = qkv.shape
    tq = bucket.shape[0]

    def head_block(col0):
        return pl.BlockSpec((1, s, ATTN_WIDTH), lambda h, i: (i, 0, col0 + h))

    return pl.pallas_call(
        functools.partial(_attn_kernel, out_scale=out_scale),
        grid=(ATTN_HEADS, b),
        in_specs=[pl.BlockSpec(memory_space=pltpu.SMEM), pl.BlockSpec(memory_space=pltpu.SMEM),
                  _const_spec(bucket.shape), head_block(0), head_block(ATTN_HEADS), head_block(2 * ATTN_HEADS),
                  _const_spec((1, ATTN_WIDTH))],
        out_specs=head_block(0),
        out_shape=jax.ShapeDtypeStruct((b, s, ATTN_DIM), F32),
        scratch_shapes=[pltpu.VMEM(bucket.shape, F32), pltpu.VMEM((2 * tq, s), F32), pltpu.VMEM((2 * tq, s), BF16),
                        pltpu.VMEM((s, 2 * ATTN_WIDTH), BF16)],
        compiler_params=_cparams(("arbitrary", "arbitrary")),
        name="diff_attn",
    )(lam, tbl, bucket, qkv, qkv, qkv, subln)


def _proj_out_kernel(x_ref, a_ref, w_ref, gpost_ref, o_ref):
    m = jnp.dot(a_ref[...].astype(BF16), w_ref[...], preferred_element_type=F32)
    o_ref[...] = x_ref[...] + _rms(m, gpost_ref[...])


def _proj_out(x2, a2, w, g_post):
    n = x2.shape[0]
    k = a2.shape[1]
    tm = _token_tile(n, 512)
    return pl.pallas_call(
        _proj_out_kernel,
        grid=(n // tm,),
        in_specs=[pl.BlockSpec((tm, D_MODEL), lambda i: (i, 0)), pl.BlockSpec((tm, k), lambda i: (i, 0)),
                  _const_spec((k, D_MODEL)), _const_spec((1, D_MODEL))],
        out_specs=pl.BlockSpec((tm, D_MODEL), lambda i: (i, 0)),
        out_shape=jax.ShapeDtypeStruct((n, D_MODEL), F32),
        compiler_params=_cparams(("parallel",)),
        name="proj_out",
    )(x2, a2, w, g_post)


def _relative_bucket(rel):
    half = N_BUCKETS // 2
    max_exact = half // 2
    ret = jnp.where(rel > 0, half, 0)
    n = jnp.abs(rel)
    nf = jnp.maximum(n, 1).astype(F32)
    large = max_exact + (jnp.log(nf / max_exact) / math.log(MAX_DISTANCE / max_exact) * (half - max_exact)).astype(jnp.int32)
    large = jnp.minimum(large, half - 1)
    return ret + jnp.where(n < max_exact, n, large)


def _bucket_slab(tq):
    a = jnp.arange(tq)[:, None]
    c = jnp.arange((2 * SLAB_REACH + 1) * tq)[None, :]
    return _relative_bucket(c - a - SLAB_REACH * tq).astype(jnp.int32)


def _row(v):
    return v.reshape(1, -1).astype(F32)


def _mamba_layer(x, pre_g, post_g, w_in, conv_w, conv_b, dt_bias, a_log, d_skip, norm_g, w_out):
    b, s, _ = x.shape
    wz = w_in[:, :D_INNER].astype(BF16)
    wx = w_in[:, D_INNER:D_INNER + CONV_DIM].astype(BF16)
    wdt = w_in[:, D_INNER + CONV_DIM:].astype(BF16)
    z, xbc, dt, dtt = _ssm_in(x, _row(pre_g), wz, wx, wdt, wdt.T)
    xs, bm, cm = _conv(xbc, conv_w.astype(F32), _row(conv_b))
    a = -jnp.exp(a_log.astype(F32))
    brow = dt_bias.astype(F32)
    e = jnp.repeat(jnp.eye(SSM_HEADS, dtype=BF16), SSM_HEAD_DIM, axis=1)
    dskip = jnp.repeat(d_skip.astype(F32), SSM_HEAD_DIM).reshape(1, D_INNER)
    common = (xs, bm, cm, dt, dtt, brow, brow[:, :, None], a, a[:, :, None], e)
    y_fwd = _ssd(*common, reverse=False)
    y = _ssd(*common, y_fwd, dskip, reverse=True)
    out = _ssm_out(x.reshape(b * s, D_MODEL), y.reshape(b * s, D_INNER), z.reshape(b * s, D_INNER),
                   _row(norm_g), w_out.astype(BF16), _row(post_g))
    return out.reshape(b, s, D_MODEL)


def _attn_layer(x, pre_g, post_g, w_qkv, lam, subln_g, w_out, rel_bias, lambda_init):
    b, s, _ = x.shape
    scale = ATTN_HEAD_DIM ** -0.5 * LOG2E
    col_scale = jnp.concatenate([jnp.full((ATTN_DIM,), scale, F32), jnp.ones((2 * ATTN_DIM,), F32)])
    w = (w_qkv.astype(F32) * col_scale).astype(BF16)
    qkv = _qkv(x.reshape(b * s, D_MODEL), _row(pre_g), w).reshape(b, s, 3 * ATTN_DIM)
    lf = lam.astype(F32)
    lam_full = jnp.exp(jnp.sum(lf[0] * lf[1])) - jnp.exp(jnp.sum(lf[2] * lf[3])) + lambda_init
    tq = _token_tile(s, ATTN_Q_TILE)
    assert tq >= MAX_DISTANCE, "bias tiles two q tiles away must lie in the saturated buckets"
    tbl = rel_bias.astype(F32).T * LOG2E
    o = _attention(lam_full.reshape(1), tbl, _bucket_slab(tq), qkv, _row(subln_g), out_scale=1.0 - lambda_init)
    out = _proj_out(x.reshape(b * s, D_MODEL), o.reshape(b * s, ATTN_DIM), w_out.astype(BF16), _row(post_g))
    return out.reshape(b, s, D_MODEL)


def _trunk(x, norm_pre, norm_post, ffn_w_gate, ffn_w_up, ffn_w_down, ssm_w_in, ssm_conv_w, ssm_conv_b,
           ssm_dt_bias, ssm_a_log, ssm_d, ssm_norm, ssm_w_out, attn_w_qkv, attn_lambda, attn_subln,
           attn_w_out, rel_bias):
    b, s, _ = x.shape
    depth = norm_pre.shape[0]

    def ffn(x, i, k, slot):
        y = _ffn(x.reshape(b * s, D_MODEL), _row(norm_pre[i, slot]), ffn_w_gate[i, k].astype(BF16),
                 ffn_w_up[i, k].astype(BF16), ffn_w_down[i, k].astype(BF16), _row(norm_post[i, slot]))
        return y.reshape(b, s, D_MODEL)

    for i in range(depth):
        x = ffn(x, i, 0, 0)
        j = i // 2
        if i % 2 == 0:
            x = _mamba_layer(x, norm_pre[i, 1], norm_post[i, 1], ssm_w_in[j], ssm_conv_w[j], ssm_conv_b[j],
                             ssm_dt_bias[j], ssm_a_log[j], ssm_d[j], ssm_norm[j], ssm_w_out[j])
        else:
            lambda_init = 0.8 - 0.6 * math.exp(-0.3 * i)
            x = _attn_layer(x, norm_pre[i, 1], norm_post[i, 1], attn_w_qkv[j], attn_lambda[j], attn_subln[j],
                            attn_w_out[j], rel_bias, lambda_init)
        x = ffn(x, i, 1, 2)
    return x


def kernel(x_prompt, x_sample, norm_pre, norm_post, ffn_w_gate, ffn_w_up, ffn_w_down, ssm_w_in, ssm_conv_w, ssm_conv_b, ssm_dt_bias, ssm_a_log, ssm_d, ssm_norm, ssm_w_out, attn_w_qkv, attn_lambda, attn_subln, attn_w_out, rel_bias):
    params = (norm_pre, norm_post, ffn_w_gate, ffn_w_up, ffn_w_down, ssm_w_in, ssm_conv_w, ssm_conv_b,
              ssm_dt_bias, ssm_a_log, ssm_d, ssm_norm, ssm_w_out, attn_w_qkv, attn_lambda, attn_subln,
              attn_w_out, rel_bias)
    return (_trunk(x_prompt, *params), _trunk(x_sample, *params))
```

```python
import functools
import math

import jax
import jax.numpy as jnp
from jax import lax
from jax.experimental import pallas as pl
from jax.experimental.pallas import tpu as pltpu

F32 = jnp.float32
BF16 = jnp.bfloat16

EPS = 1e-6
D_MODEL = 1024
D_FF = 2816
FF_CHUNK = 256

D_INNER = 2048
SSM_HEAD_DIM = 64
SSM_HEADS = 32
SSM_GROUPS = 4
HEADS_PER_GROUP = SSM_HEADS // SSM_GROUPS
D_STATE = 128
D_CONV = 5
CHUNK = 128
BC_DIM = SSM_GROUPS * D_STATE
CONV_DIM = D_INNER + 2 * BC_DIM
GROUP_WIDTH = D_INNER // SSM_GROUPS

ATTN_HEADS = 8
ATTN_HEAD_DIM = 64
ATTN_WIDTH = 2 * ATTN_HEAD_DIM
ATTN_DIM = ATTN_HEADS * ATTN_WIDTH
N_BUCKETS = 32
MAX_DISTANCE = 128
ATTN_Q_TILE = 512
SLAB_REACH = 2
LOG2E = math.log2(math.e)

HALO = 8
VMEM_LIMIT = 56 * 1024 * 1024

NT_DIMS = (((1,), (1,)), ((), ()))
TN_DIMS = (((0,), (0,)), ((), ()))


def _cparams(sem):
    return pltpu.CompilerParams(dimension_semantics=sem, vmem_limit_bytes=VMEM_LIMIT)


def _rms(x, g):
    return x * lax.rsqrt(jnp.mean(x * x, axis=-1, keepdims=True) + EPS) * g


def _silu(x):
    return x * (1.0 / (1.0 + jnp.exp(-x)))


def _const_spec(shape):
    nd = len(shape)
    return pl.BlockSpec(shape, lambda *_: (0,) * nd, pipeline_mode=pl.Buffered(1))


def _token_tile(n, pref):
    t = min(pref, n)
    while n % t:
        t //= 2
    return t


def _ffn_kernel(x_ref, gpre_ref, wg_ref, wu_ref, wd_ref, gpost_ref, o_ref, xn_ref, acc_ref):
    xn_ref[...] = _rms(x_ref[...], gpre_ref[...]).astype(BF16)
    for c in range(D_FF // FF_CHUNK):
        sl = slice(c * FF_CHUNK, (c + 1) * FF_CHUNK)
        xn = xn_ref[...]
        g = jnp.dot(xn, wg_ref[:, sl], preferred_element_type=F32)
        u = jnp.dot(xn, wu_ref[:, sl], preferred_element_type=F32)
        h = (_silu(g) * u).astype(BF16)
        d = jnp.dot(h, wd_ref[sl, :], preferred_element_type=F32)
        if c == 0:
            acc_ref[...] = d
        else:
            acc_ref[...] += d
    o_ref[...] = x_ref[...] + 0.5 * _rms(acc_ref[...], gpost_ref[...])


def _ffn(x2, g_pre, wg, wu, wd, g_post):
    n = x2.shape[0]
    tm = _token_tile(n, 512)
    row = pl.BlockSpec((tm, D_MODEL), lambda i: (i, 0))
    return pl.pallas_call(
        _ffn_kernel,
        grid=(n // tm,),
        in_specs=[row, _const_spec((1, D_MODEL)), _const_spec((D_MODEL, D_FF)), _const_spec((D_MODEL, D_FF)),
                  _const_spec((D_FF, D_MODEL)), _const_spec((1, D_MODEL))],
        out_specs=row,
        out_shape=jax.ShapeDtypeStruct((n, D_MODEL), F32),
        scratch_shapes=[pltpu.VMEM((tm, D_MODEL), BF16), pltpu.VMEM((tm, D_MODEL), F32)],
        compiler_params=_cparams(("parallel",)),
        name="ffn",
    )(x2, g_pre, wg, wu, wd, g_post)


def _ssm_in_kernel(x_ref, g_ref, wz_ref, wx_ref, wdt_ref, wdtt_ref, z_ref, xbc_ref, dt_ref, dtt_ref):
    u = _rms(x_ref[0], g_ref[...]).astype(BF16)
    z_ref[0] = jnp.dot(u, wz_ref[...], preferred_element_type=F32)
    xbc_ref[0] = jnp.dot(u, wx_ref[...], preferred_element_type=F32)
    dt_ref[0] = jnp.dot(u, wdt_ref[...], preferred_element_type=F32)
    dtt_ref[0] = lax.dot_general(wdtt_ref[...], u, NT_DIMS, preferred_element_type=F32)


def _ssm_in(x, g, wz, wx, wdt, wdtt):
    b, s, _ = x.shape
    tm = _token_tile(s, 512)
    ndt = 2 * SSM_HEADS

    def tok(width):
        return pl.BlockSpec((1, tm, width), lambda i, j: (i, j, 0))

    return pl.pallas_call(
        _ssm_in_kernel,
        grid=(b, s // tm),
        in_specs=[tok(D_MODEL), _const_spec((1, D_MODEL)), _const_spec((D_MODEL, D_INNER)),
                  _const_spec((D_MODEL, CONV_DIM)), _const_spec((D_MODEL, ndt)), _const_spec((ndt, D_MODEL))],
        out_specs=[tok(D_INNER), tok(CONV_DIM), tok(ndt), pl.BlockSpec((1, ndt, tm), lambda i, j: (i, 0, j))],
        out_shape=[jax.ShapeDtypeStruct((b, s, D_INNER), F32), jax.ShapeDtypeStruct((b, s, CONV_DIM), F32),
                   jax.ShapeDtypeStruct((b, s, ndt), F32), jax.ShapeDtypeStruct((b, ndt, s), F32)],
        compiler_params=_cparams(("parallel", "parallel")),
        name="ssm_in",
    )(x, g, wz, wx, wdt, wdtt)


def _conv_kernel(prev_ref, main_ref, next_ref, w_ref, b_ref, xs_ref, bm_ref, cm_ref, ext_ref):
    j = pl.program_id(1)
    tc = main_ref.shape[1]
    ext_ref[pl.ds(0, HALO), :] = jnp.where(j > 0, prev_ref[0], 0.0)
    ext_ref[pl.ds(HALO, tc), :] = main_ref[0]
    ext_ref[pl.ds(HALO + tc, HALO), :] = jnp.where(j < pl.num_programs(1) - 1, next_ref[0], 0.0)
    pad = D_CONV // 2
    acc = b_ref[...] + w_ref[0:1, :] * ext_ref[pl.ds(HALO - pad, tc), :]
    for k in range(1, D_CONV):
        acc = acc + w_ref[k:k + 1, :] * ext_ref[pl.ds(HALO - pad + k, tc), :]
    y = _silu(acc)
    xs_ref[0] = y[:, :D_INNER]
    bm_ref[0] = y[:, D_INNER:D_INNER + BC_DIM]
    cm_ref[0] = y[:, D_INNER + BC_DIM:]


def _conv(xbc, w, bias):
    b, s, _ = xbc.shape
    tc = _token_tile(s, 256)
    nb = tc // HALO
    last = s // HALO - 1

    def tok(width):
        return pl.BlockSpec((1, tc, width), lambda i, j: (i, j, 0))

    return pl.pallas_call(
        _conv_kernel,
        grid=(b, s // tc),
        in_specs=[pl.BlockSpec((1, HALO, CONV_DIM), lambda i, j: (i, jnp.maximum(j * nb - 1, 0), 0)),
                  tok(CONV_DIM),
                  pl.BlockSpec((1, HALO, CONV_DIM), lambda i, j: (i, jnp.minimum((j + 1) * nb, last), 0)),
                  _const_spec((D_CONV, CONV_DIM)), _const_spec((1, CONV_DIM))],
        out_specs=[tok(D_INNER), tok(BC_DIM), tok(BC_DIM)],
        out_shape=[jax.ShapeDtypeStruct((b, s, D_INNER), F32), jax.ShapeDtypeStruct((b, s, BC_DIM), F32),
                   jax.ShapeDtypeStruct((b, s, BC_DIM), F32)],
        scratch_shapes=[pltpu.VMEM((tc + 2 * HALO, CONV_DIM), F32)],
        compiler_params=_cparams(("parallel", "parallel")),
        name="ssm_conv",
    )(xbc, xbc, xbc, w, bias)


def _split3(v):
    hi = v.astype(BF16)
    r1 = v - hi.astype(F32)
    mid = r1.astype(BF16)
    lo = (r1 - mid.astype(F32)).astype(BF16)
    return hi, mid, lo


def _cumsum_rows(x, lower):
    return sum(jnp.dot(lower, p, preferred_element_type=F32) for p in _split3(x))


def _cumsum_lanes(x, upper):
    return sum(jnp.dot(p, upper, preferred_element_type=F32) for p in _split3(x))


def _expand_heads(v, e):
    hi = v.astype(BF16)
    lo = (v - hi.astype(F32)).astype(BF16)
    return jnp.dot(hi, e, preferred_element_type=F32) + jnp.dot(lo, e, preferred_element_type=F32)


def _ssd_kernel(*refs, reverse, final):
    if final:
        (xs_ref, bm_ref, cm_ref, dt_ref, dtt_ref, brow_ref, bcol_ref, arow_ref, acol_ref, e_ref,
         prev_ref, dskip_ref, y_ref, st_ref) = refs
    else:
        (xs_ref, bm_ref, cm_ref, dt_ref, dtt_ref, brow_ref, bcol_ref, arow_ref, acol_ref, e_ref,
         y_ref, st_ref) = refs
    d = 1 if reverse else 0
    hs = slice(d * SSM_HEADS, (d + 1) * SSM_HEADS)

    @pl.when(pl.program_id(1) == 0)
    def _():
        st_ref[...] = jnp.zeros_like(st_ref)

    dt_c = jax.nn.softplus(dt_ref[0, :, hs] + brow_ref[d:d + 1, :])
    dt_r = jax.nn.softplus(dtt_ref[0, hs, :] + bcol_ref[d])
    da_c = dt_c * arow_ref[d:d + 1, :]
    da_r = dt_r * acol_ref[d]
    ii = lax.broadcasted_iota(jnp.int32, (CHUNK, CHUNK), 0)
    jj = lax.broadcasted_iota(jnp.int32, (CHUNK, CHUNK), 1)
    cs_c = _cumsum_rows(da_c, jnp.where(jj <= ii, 1.0, 0.0).astype(BF16))
    cs_r = _cumsum_lanes(da_r, jnp.where(ii <= jj, 1.0, 0.0).astype(BF16))
    tot = cs_c[CHUNK - 1:CHUNK, :]
    if reverse:
        a_c = da_c - cs_c
        a_r = da_r - cs_r
        off_scale = jnp.exp(tot + a_c)
        st_w = jnp.exp(-a_c) * dt_c
    else:
        a_c = cs_c
        a_r = cs_r
        off_scale = jnp.exp(cs_c)
        st_w = jnp.exp(tot - cs_c) * dt_c
    e = e_ref[...]
    off_x = _expand_heads(off_scale, e)
    stw_x = _expand_heads(st_w, e)
    dec_x = _expand_heads(jnp.broadcast_to(jnp.exp(tot), (HALO, SSM_HEADS)), e)[0:1, :]

    keep = (jj >= ii) if reverse else (ii >= jj)
    first_head = jj < SSM_HEAD_DIM

    for g in range(SSM_GROUPS):
        gs = slice(g * D_STATE, (g + 1) * D_STATE)
        ws = slice(g * GROUP_WIDTH, (g + 1) * GROUP_WIDTH)
        bg = bm_ref[0, :, gs].astype(BF16)
        cg = cm_ref[0, :, gs].astype(BF16)
        cb = lax.dot_general(cg, bg, NT_DIMS, preferred_element_type=F32)
        st = st_ref[:, ws]
        y_off = jnp.dot(cg, st.astype(BF16), preferred_element_type=F32) * off_x[:, ws]
        xs_g = xs_ref[0, :, ws]
        xw = (xs_g * stw_x[:, ws]).astype(BF16)
        st_ref[:, ws] = st * dec_x[:, ws] + lax.dot_general(bg, xw, TN_DIMS, preferred_element_type=F32)
        for pr in range(HEADS_PER_GROUP // 2):
            h0 = g * HEADS_PER_GROUP + 2 * pr
            ps = slice(h0 * SSM_HEAD_DIM, (h0 + 2) * SSM_HEAD_DIM)
            x_p = xs_ref[0, :, ps]
            x_pb = x_p.astype(BF16)
            ys = []
            for h in (h0, h0 + 1):
                seg = a_c[:, h:h + 1] - a_r[h:h + 1, :]
                w = cb * jnp.exp(jnp.where(keep, seg, -jnp.inf)) * dt_r[h:h + 1, :]
                ys.append(jnp.dot(w.astype(BF16), x_pb, preferred_element_type=F32))
            y_p = jnp.where(first_head, ys[0], ys[1]) + y_off[:, pr * 2 * SSM_HEAD_DIM:(pr + 1) * 2 * SSM_HEAD_DIM]
            if final:
                y_p = y_p + prev_ref[0, :, ps] + x_p * dskip_ref[:, ps]
            y_ref[0, :, ps] = y_p


def _ssd(xs, bm, cm, dt, dtt, brow, bcol, arow, acol, e, prev=None, dskip=None, *, reverse):
    b, s, _ = xs.shape
    nc = s // CHUNK
    final = prev is not None
    ndt = 2 * SSM_HEADS

    def cidx(j):
        return nc - 1 - j if reverse else j

    def tok(width):
        return pl.BlockSpec((1, CHUNK, width), lambda i, j: (i, cidx(j), 0))

    in_specs = [tok(D_INNER), tok(BC_DIM), tok(BC_DIM), tok(ndt),
                pl.BlockSpec((1, ndt, CHUNK), lambda i, j: (i, 0, cidx(j))),
                _const_spec((2, SSM_HEADS)), _const_spec((2, SSM_HEADS, 1)),
                _const_spec((2, SSM_HEADS)), _const_spec((2, SSM_HEADS, 1)),
                _const_spec((SSM_HEADS, D_INNER))]
    args = [xs, bm, cm, dt, dtt, brow, bcol, arow, acol, e]
    if final:
        in_specs += [tok(D_INNER), _const_spec((1, D_INNER))]
        args += [prev, dskip]
    return pl.pallas_call(
        functools.partial(_ssd_kernel, reverse=reverse, final=final),
        grid=(b, nc),
        in_specs=in_specs,
        out_specs=tok(D_INNER),
        out_shape=jax.ShapeDtypeStruct((b, s, D_INNER), F32),
        scratch_shapes=[pltpu.VMEM((D_STATE, D_INNER), F32)],
        compiler_params=_cparams(("parallel", "arbitrary")),
        name="ssd_bwd" if reverse else "ssd_fwd",
    )(*args)


def _ssm_out_kernel(x_ref, y_ref, z_ref, ng_ref, w_ref, gpost_ref, o_ref):
    y = y_ref[...] * _silu(z_ref[...])
    parts = []
    for g in range(SSM_GROUPS):
        yg = y[:, g * GROUP_WIDTH:(g + 1) * GROUP_WIDTH]
        parts.append(yg * lax.rsqrt(jnp.mean(yg * yg, axis=-1, keepdims=True) + EPS))
    yn = (jnp.concatenate(parts, axis=-1) * ng_ref[...]).astype(BF16)
    m = jnp.dot(yn, w_ref[...], preferred_element_type=F32)
    o_ref[...] = x_ref[...] + _rms(m, gpost_ref[...])


def _ssm_out(x2, y2, z2, ng, w, g_post):
    n = x2.shape[0]
    tm = _token_tile(n, 512)

    def row(width):
        return pl.BlockSpec((tm, width), lambda i: (i, 0))

    return pl.pallas_call(
        _ssm_out_kernel,
        grid=(n // tm,),
        in_specs=[row(D_MODEL), row(D_INNER), row(D_INNER), _const_spec((1, D_INNER)),
                  _const_spec((D_INNER, D_MODEL)), _const_spec((1, D_MODEL))],
        out_specs=row(D_MODEL),
        out_shape=jax.ShapeDtypeStruct((n, D_MODEL), F32),
        compiler_params=_cparams(("parallel",)),
        name="ssm_out",
    )(x2, y2, z2, ng, w, g_post)


def _qkv_kernel(x_ref, g_ref, w_ref, o_ref):
    u = _rms(x_ref[...], g_ref[...]).astype(BF16)
    o_ref[...] = jnp.dot(u, w_ref[...], preferred_element_type=F32).astype(BF16)


def _qkv(x2, g, w):
    n = x2.shape[0]
    tm = _token_tile(n, 512)
    return pl.pallas_call(
        _qkv_kernel,
        grid=(n // tm,),
        in_specs=[pl.BlockSpec((tm, D_MODEL), lambda i: (i, 0)), _const_spec((1, D_MODEL)),
                  _const_spec((D_MODEL, 3 * ATTN_DIM))],
        out_specs=pl.BlockSpec((tm, 3 * ATTN_DIM), lambda i: (i, 0)),
        out_shape=jax.ShapeDtypeStruct((n, 3 * ATTN_DIM), BF16),
        compiler_params=_cparams(("parallel",)),
        name="attn_qkv",
    )(x2, g, w)


def _attn_kernel(lam_ref, tbl_ref, bucket_ref, q_ref, k_ref, v_ref, g_ref, o_ref, slab_ref, s_ref, p_ref, vv_ref, *,
                 out_scale):
    tq = slab_ref.shape[0]
    s = k_ref.shape[1]
    nk = s // tq
    h = pl.program_id(0)

    @pl.when(pl.program_id(1) == 0)
    def _():
        bk = bucket_ref[...]
        acc = jnp.zeros(bk.shape, F32)
        for kk in range(N_BUCKETS):
            acc = jnp.where(bk == kk, tbl_ref[h, kk], acc)
        slab_ref[...] = acc

    lam = lam_ref[0]
    g = g_ref[...]
    vv_ref[...] = jnp.concatenate([v_ref[0], jnp.ones((s, ATTN_WIDTH), BF16)], axis=1)
    kchunk = 2 * tq if s % (2 * tq) == 0 else tq

    def q_tile(qi, carry):
        q0 = pl.multiple_of(qi * tq, tq)
        q = q_ref[0, pl.ds(q0, tq), :]
        lane = lax.broadcasted_iota(jnp.int32, q.shape, 1)
        zero = jnp.zeros_like(q)
        q2 = jnp.concatenate([jnp.where(lane < ATTN_HEAD_DIM, q, zero), jnp.where(lane >= ATTN_HEAD_DIM, q, zero)],
                             axis=0)
        mrun = jnp.full((2 * tq, 128), -jnp.inf, F32)
        for k0 in range(0, s, kchunk):
            tiles = []
            for kc in range(k0 // tq, (k0 + kchunk) // tq):
                d = jnp.clip(kc - qi, -SLAB_REACH, SLAB_REACH) + SLAB_REACH
                tiles.append(slab_ref[:, pl.ds(pl.multiple_of(d * tq, tq), tq)])
            bias = jnp.concatenate(tiles, axis=1) if len(tiles) > 1 else tiles[0]
            sc = lax.dot_general(q2, k_ref[0, k0:k0 + kchunk, :], NT_DIMS, preferred_element_type=F32)
            sc = sc + jnp.concatenate([bias, bias], axis=0)
            s_ref[:, k0:k0 + kchunk] = sc
            for j in range(kchunk // 128):
                mrun = jnp.maximum(mrun, sc[:, j * 128:(j + 1) * 128])
        mx = jnp.max(mrun, axis=-1, keepdims=True)
        p_ref[...] = jnp.exp2(s_ref[...] - mx).astype(BF16)
        vv = vv_ref[...]
        acc1 = jnp.dot(p_ref[:tq, :], vv, preferred_element_type=F32)
        acc2 = jnp.dot(p_ref[tq:, :], vv, preferred_element_type=F32)
        o1 = acc1[:, :ATTN_WIDTH] / acc1[:, ATTN_WIDTH:ATTN_WIDTH + 1]
        o2 = acc2[:, :ATTN_WIDTH] / acc2[:, ATTN_WIDTH:ATTN_WIDTH + 1]
        o_ref[0, pl.ds(q0, tq), :] = _rms(o1 - lam * o2, g) * out_scale
        return carry

    lax.fori_loop(0, s // tq, q_tile, 0, unroll=2)


def _attention(lam, tbl, bucket, qkv, subln, *, out_scale):
    b, s, _ = qkv.shape
    tq = bucket.shape[0]

    def head_block(col0):
        return pl.BlockSpec((1, s, ATTN_WIDTH), lambda h, i: (i, 0, col0 + h))

    return pl.pallas_call(
        functools.partial(_attn_kernel, out_scale=out_scale),
        grid=(ATTN_HEADS, b),
        in_specs=[pl.BlockSpec(memory_space=pltpu.SMEM), pl.BlockSpec(memory_space=pltpu.SMEM),
                  _const_spec(bucket.shape), head_block(0), head_block(ATTN_HEADS), head_block(2 * ATTN_HEADS),
                  _const_spec((1, ATTN_WIDTH))],
        out_specs=head_block(0),
        out_shape=jax.ShapeDtypeStruct((b, s, ATTN_DIM), F32),
        scratch_shapes=[pltpu.VMEM(bucket.shape, F32), pltpu.VMEM((2 * tq, s), F32), pltpu.VMEM((2 * tq, s), BF16),
                        pltpu.VMEM((s, 2 * ATTN_WIDTH), BF16)],
        compiler_params=_cparams(("arbitrary", "arbitrary")),
        name="diff_attn",
    )(lam, tbl, bucket, qkv, qkv, qkv, subln)


def _proj_out_kernel(x_ref, a_ref, w_ref, gpost_ref, o_ref):
    m = jnp.dot(a_ref[...].astype(BF16), w_ref[...], preferred_element_type=F32)
    o_ref[...] = x_ref[...] + _rms(m, gpost_ref[...])


def _proj_out(x2, a2, w, g_post):
    n = x2.shape[0]
    k = a2.shape[1]
    tm = _token_tile(n, 512)
    return pl.pallas_call(
        _proj_out_kernel,
        grid=(n // tm,),
        in_specs=[pl.BlockSpec((tm, D_MODEL), lambda i: (i, 0)), pl.BlockSpec((tm, k), lambda i: (i, 0)),
                  _const_spec((k, D_MODEL)), _const_spec((1, D_MODEL))],
        out_specs=pl.BlockSpec((tm, D_MODEL), lambda i: (i, 0)),
        out_shape=jax.ShapeDtypeStruct((n, D_MODEL), F32),
        compiler_params=_cparams(("parallel",)),
        name="proj_out",
    )(x2, a2, w, g_post)


def _relative_bucket(rel):
    half = N_BUCKETS // 2
    max_exact = half // 2
    ret = jnp.where(rel > 0, half, 0)
    n = jnp.abs(rel)
    nf = jnp.maximum(n, 1).astype(F32)
    large = max_exact + (jnp.log(nf / max_exact) / math.log(MAX_DISTANCE / max_exact) * (half - max_exact)).astype(jnp.int32)
    large = jnp.minimum(large, half - 1)
    return ret + jnp.where(n < max_exact, n, large)


def _bucket_slab(tq):
    a = jnp.arange(tq)[:, None]
    c = jnp.arange((2 * SLAB_REACH + 1) * tq)[None, :]
    return _relative_bucket(c - a - SLAB_REACH * tq).astype(jnp.int32)


def _row(v):
    return v.reshape(1, -1).astype(F32)


def _mamba_layer(x, pre_g, post_g, w_in, conv_w, conv_b, dt_bias, a_log, d_skip, norm_g, w_out):
    b, s, _ = x.shape
    wz = w_in[:, :D_INNER].astype(BF16)
    wx = w_in[:, D_INNER:D_INNER + CONV_DIM].astype(BF16)
    wdt = w_in[:, D_INNER + CONV_DIM:].astype(BF16)
    z, xbc, dt, dtt = _ssm_in(x, _row(pre_g), wz, wx, wdt, wdt.T)
    xs, bm, cm = _conv(xbc, conv_w.astype(F32), _row(conv_b))
    a = -jnp.exp(a_log.astype(F32))
    brow = dt_bias.astype(F32)
    e = jnp.repeat(jnp.eye(SSM_HEADS, dtype=BF16), SSM_HEAD_DIM, axis=1)
    dskip = jnp.repeat(d_skip.astype(F32), SSM_HEAD_DIM).reshape(1, D_INNER)
    common = (xs, bm, cm, dt, dtt, brow, brow[:, :, None], a, a[:, :, None], e)
    y_fwd = _ssd(*common, reverse=False)
    y = _ssd(*common, y_fwd, dskip, reverse=True)
    out = _ssm_out(x.reshape(b * s, D_MODEL), y.reshape(b * s, D_INNER), z.reshape(b * s, D_INNER),
                   _row(norm_g), w_out.astype(BF16), _row(post_g))
    return out.reshape(b, s, D_MODEL)


def _attn_layer(x, pre_g, post_g, w_qkv, lam, subln_g, w_out, rel_bias, lambda_init):
    b, s, _ = x.shape
    scale = ATTN_HEAD_DIM ** -0.5 * LOG2E
    col_scale = jnp.concatenate([jnp.full((ATTN_DIM,), scale, F32), jnp.ones((2 * ATTN_DIM,), F32)])
    w = (w_qkv.astype(F32) * col_scale).astype(BF16)
    qkv = _qkv(x.reshape(b * s, D_MODEL), _row(pre_g), w).reshape(b, s, 3 * ATTN_DIM)
    lf = lam.astype(F32)
    lam_full = jnp.exp(jnp.sum(lf[0] * lf[1])) - jnp.exp(jnp.sum(lf[2] * lf[3])) + lambda_init
    tq = _token_tile(s, ATTN_Q_TILE)
    assert tq >= MAX_DISTANCE, "bias tiles two q tiles away must lie in the saturated buckets"
    tbl = rel_bias.astype(F32).T * LOG2E
    o = _attention(lam_full.reshape(1), tbl, _bucket_slab(tq), qkv, _row(subln_g), out_scale=1.0 - lambda_init)
    out = _proj_out(x.reshape(b * s, D_MODEL), o.reshape(b * s, ATTN_DIM), w_out.astype(BF16), _row(post_g))
    return out.reshape(b, s, D_MODEL)


def _trunk(x, norm_pre, norm_post, ffn_w_gate, ffn_w_up, ffn_w_down, ssm_w_in, ssm_conv_w, ssm_conv_b,
           ssm_dt_bias, ssm_a_log, ssm_d, ssm_norm, ssm_w_out, attn_w_qkv, attn_lambda, attn_subln,
           attn_w_out, rel_bias):
    b, s, _ = x.shape
    depth = norm_pre.shape[0]

    def ffn(x, i, k, slot):
        y = _ffn(x.reshape(b * s, D_MODEL), _row(norm_pre[i, slot]), ffn_w_gate[i, k].astype(BF16),
                 ffn_w_up[i, k].astype(BF16), ffn_w_down[i, k].astype(BF16), _row(norm_post[i, slot]))
        return y.reshape(b, s, D_MODEL)

    for i in range(depth):
        x = ffn(x, i, 0, 0)
        j = i // 2
        if i % 2 == 0:
            x = _mamba_layer(x, norm_pre[i, 1], norm_post[i, 1], ssm_w_in[j], ssm_conv_w[j], ssm_conv_b[j],
                             ssm_dt_bias[j], ssm_a_log[j], ssm_d[j], ssm_norm[j], ssm_w_out[j])
        else:
            lambda_init = 0.8 - 0.6 * math.exp(-0.3 * i)
            x = _attn_layer(x, norm_pre[i, 1], norm_post[i, 1], attn_w_qkv[j], attn_lambda[j], attn_subln[j],
                            attn_w_out[j], rel_bias, lambda_init)
        x = ffn(x, i, 1, 2)
    return x


def kernel(x_prompt, x_sample, norm_pre, norm_post, ffn_w_gate, ffn_w_up, ffn_w_down, ssm_w_in, ssm_conv_w, ssm_conv_b, ssm_dt_bias, ssm_a_log, ssm_d, ssm_norm, ssm_w_out, attn_w_qkv, attn_lambda, attn_subln, attn_w_out, rel_bias):
    params = (norm_pre, norm_post, ffn_w_gate, ffn_w_up, ffn_w_down, ssm_w_in, ssm_conv_w, ssm_conv_b,
              ssm_dt_bias, ssm_a_log, ssm_d, ssm_norm, ssm_w_out, attn_w_qkv, attn_lambda, attn_subln,
              attn_w_out, rel_bias)
    return (_trunk(x_prompt, *params), _trunk(x_sample, *params))
```

```python
import functools
import math

import jax
import jax.numpy as jnp
from jax import lax
from jax.experimental import pallas as pl
from jax.experimental.pallas import tpu as pltpu

F32 = jnp.float32
BF16 = jnp.bfloat16

EPS = 1e-6
D_MODEL = 1024
D_FF = 2816
FF_CHUNK = 256

D_INNER = 2048
SSM_HEAD_DIM = 64
SSM_HEADS = 32
SSM_GROUPS = 4
HEADS_PER_GROUP = SSM_HEADS // SSM_GROUPS
D_STATE = 128
D_CONV = 5
CHUNK = 128
BC_DIM = SSM_GROUPS * D_STATE
CONV_DIM = D_INNER + 2 * BC_DIM
GROUP_WIDTH = D_INNER // SSM_GROUPS

ATTN_HEADS = 8
ATTN_HEAD_DIM = 64
ATTN_WIDTH = 2 * ATTN_HEAD_DIM
ATTN_DIM = ATTN_HEADS * ATTN_WIDTH
N_BUCKETS = 32
MAX_DISTANCE = 128
ATTN_Q_TILE = 512
ATTN_ROW_BLOCK = 256
SLAB_REACH = 2
LOG2E = math.log2(math.e)

HALO = 16
VMEM_LIMIT = 56 * 1024 * 1024

NT_DIMS = (((1,), (1,)), ((), ()))
TN_DIMS = (((0,), (0,)), ((), ()))


def _cparams(sem):
    return pltpu.CompilerParams(dimension_semantics=sem, vmem_limit_bytes=VMEM_LIMIT)


def _rms(x, g):
    return x * lax.rsqrt(jnp.mean(x * x, axis=-1, keepdims=True) + EPS) * g


def _silu(x):
    return x * (1.0 / (1.0 + jnp.exp(-x)))


def _const_spec(shape):
    nd = len(shape)
    return pl.BlockSpec(shape, lambda *_: (0,) * nd, pipeline_mode=pl.Buffered(1))


def _token_tile(n, pref):
    t = min(pref, n)
    while n % t:
        t //= 2
    return t


def _ffn_kernel(x_ref, gpre_ref, wg_ref, wu_ref, wd_ref, gpost_ref, o_ref, xn_ref, acc_ref):
    xn_ref[...] = _rms(x_ref[...], gpre_ref[...]).astype(BF16)
    for c in range(D_FF // FF_CHUNK):
        sl = slice(c * FF_CHUNK, (c + 1) * FF_CHUNK)
        xn = xn_ref[...]
        g = jnp.dot(xn, wg_ref[:, sl], preferred_element_type=F32)
        u = jnp.dot(xn, wu_ref[:, sl], preferred_element_type=F32)
        h = (_silu(g) * u).astype(BF16)
        d = jnp.dot(h, wd_ref[sl, :], preferred_element_type=F32)
        if c == 0:
            acc_ref[...] = d
        else:
            acc_ref[...] += d
    o_ref[...] = x_ref[...] + 0.5 * _rms(acc_ref[...], gpost_ref[...])


def _ffn(x2, g_pre, wg, wu, wd, g_post):
    n = x2.shape[0]
    tm = _token_tile(n, 512)
    row = pl.BlockSpec((tm, D_MODEL), lambda i: (i, 0))
    return pl.pallas_call(
        _ffn_kernel,
        grid=(n // tm,),
        in_specs=[row, _const_spec((1, D_MODEL)), _const_spec((D_MODEL, D_FF)), _const_spec((D_MODEL, D_FF)),
                  _const_spec((D_FF, D_MODEL)), _const_spec((1, D_MODEL))],
        out_specs=row,
        out_shape=jax.ShapeDtypeStruct((n, D_MODEL), F32),
        scratch_shapes=[pltpu.VMEM((tm, D_MODEL), BF16), pltpu.VMEM((tm, D_MODEL), F32)],
        compiler_params=_cparams(("parallel",)),
        name="ffn",
    )(x2, g_pre, wg, wu, wd, g_post)


def _ssm_in_kernel(x_ref, g_ref, wz_ref, wx_ref, wdt_ref, wdtt_ref, z_ref, xbc_ref, dt_ref, dtt_ref):
    u = _rms(x_ref[0], g_ref[...]).astype(BF16)
    z_ref[0] = jnp.dot(u, wz_ref[...], preferred_element_type=F32).astype(BF16)
    xbc_ref[0] = jnp.dot(u, wx_ref[...], preferred_element_type=F32).astype(BF16)
    dt_ref[0] = jnp.dot(u, wdt_ref[...], preferred_element_type=F32)
    dtt_ref[0] = lax.dot_general(wdtt_ref[...], u, NT_DIMS, preferred_element_type=F32)


def _ssm_in(x, g, wz, wx, wdt, wdtt):
    b, s, _ = x.shape
    tm = _token_tile(s, 512)
    ndt = 2 * SSM_HEADS

    def tok(width):
        return pl.BlockSpec((1, tm, width), lambda i, j: (i, j, 0))

    return pl.pallas_call(
        _ssm_in_kernel,
        grid=(b, s // tm),
        in_specs=[tok(D_MODEL), _const_spec((1, D_MODEL)), _const_spec((D_MODEL, D_INNER)),
                  _const_spec((D_MODEL, CONV_DIM)), _const_spec((D_MODEL, ndt)), _const_spec((ndt, D_MODEL))],
        out_specs=[tok(D_INNER), tok(CONV_DIM), tok(ndt), pl.BlockSpec((1, ndt, tm), lambda i, j: (i, 0, j))],
        out_shape=[jax.ShapeDtypeStruct((b, s, D_INNER), BF16), jax.ShapeDtypeStruct((b, s, CONV_DIM), BF16),
                   jax.ShapeDtypeStruct((b, s, ndt), F32), jax.ShapeDtypeStruct((b, ndt, s), F32)],
        compiler_params=_cparams(("parallel", "parallel")),
        name="ssm_in",
    )(x, g, wz, wx, wdt, wdtt)


def _conv_kernel(prev_ref, main_ref, next_ref, w_ref, b_ref, xs_ref, bm_ref, cm_ref, ext_ref):
    j = pl.program_id(1)
    tc = main_ref.shape[1]
    ext_ref[pl.ds(0, HALO), :] = jnp.where(j > 0, prev_ref[0].astype(F32), 0.0)
    ext_ref[pl.ds(HALO, tc), :] = main_ref[0].astype(F32)
    ext_ref[pl.ds(HALO + tc, HALO), :] = jnp.where(j < pl.num_programs(1) - 1, next_ref[0].astype(F32), 0.0)
    pad = D_CONV // 2
    acc = b_ref[...] + w_ref[0:1, :] * ext_ref[pl.ds(HALO - pad, tc), :]
    for k in range(1, D_CONV):
        acc = acc + w_ref[k:k + 1, :] * ext_ref[pl.ds(HALO - pad + k, tc), :]
    y = _silu(acc).astype(BF16)
    xs_ref[0] = y[:, :D_INNER]
    bm_ref[0] = y[:, D_INNER:D_INNER + BC_DIM]
    cm_ref[0] = y[:, D_INNER + BC_DIM:]


def _conv(xbc, w, bias):
    b, s, _ = xbc.shape
    tc = _token_tile(s, 256)
    nb = tc // HALO
    last = s // HALO - 1

    def tok(width):
        return pl.BlockSpec((1, tc, width), lambda i, j: (i, j, 0))

    return pl.pallas_call(
        _conv_kernel,
        grid=(b, s // tc),
        in_specs=[pl.BlockSpec((1, HALO, CONV_DIM), lambda i, j: (i, jnp.maximum(j * nb - 1, 0), 0)),
                  tok(CONV_DIM),
                  pl.BlockSpec((1, HALO, CONV_DIM), lambda i, j: (i, jnp.minimum((j + 1) * nb, last), 0)),
                  _const_spec((D_CONV, CONV_DIM)), _const_spec((1, CONV_DIM))],
        out_specs=[tok(D_INNER), tok(BC_DIM), tok(BC_DIM)],
        out_shape=[jax.ShapeDtypeStruct((b, s, D_INNER), BF16), jax.ShapeDtypeStruct((b, s, BC_DIM), BF16),
                   jax.ShapeDtypeStruct((b, s, BC_DIM), BF16)],
        scratch_shapes=[pltpu.VMEM((tc + 2 * HALO, CONV_DIM), F32)],
        compiler_params=_cparams(("parallel", "parallel")),
        name="ssm_conv",
    )(xbc, xbc, xbc, w, bias)


def _split3(v):
    hi = v.astype(BF16)
    r1 = v - hi.astype(F32)
    mid = r1.astype(BF16)
    lo = (r1 - mid.astype(F32)).astype(BF16)
    return hi, mid, lo


def _cumsum_rows(x, lower):
    return sum(jnp.dot(lower, p, preferred_element_type=F32) for p in _split3(x))


def _cumsum_lanes(x, upper):
    return sum(jnp.dot(p, upper, preferred_element_type=F32) for p in _split3(x))


def _expand_heads(v, e):
    hi = v.astype(BF16)
    lo = (v - hi.astype(F32)).astype(BF16)
    return jnp.dot(hi, e, preferred_element_type=F32) + jnp.dot(lo, e, preferred_element_type=F32)


def _ssd_kernel(*refs, reverse, final):
    if final:
        (xs_ref, bm_ref, cm_ref, dt_ref, dtt_ref, brow_ref, bcol_ref, arow_ref, acol_ref, e_ref,
         prev_ref, dskip_ref, y_ref, st_ref) = refs
    else:
        (xs_ref, bm_ref, cm_ref, dt_ref, dtt_ref, brow_ref, bcol_ref, arow_ref, acol_ref, e_ref,
         y_ref, st_ref) = refs
    d = 1 if reverse else 0
    hs = slice(d * SSM_HEADS, (d + 1) * SSM_HEADS)

    @pl.when(pl.program_id(1) == 0)
    def _():
        st_ref[...] = jnp.zeros_like(st_ref)

    dt_c = jax.nn.softplus(dt_ref[0, :, hs] + brow_ref[d:d + 1, :])
    dt_r = jax.nn.softplus(dtt_ref[0, hs, :] + bcol_ref[d])
    da_c = dt_c * arow_ref[d:d + 1, :]
    da_r = dt_r * acol_ref[d]
    ii = lax.broadcasted_iota(jnp.int32, (CHUNK, CHUNK), 0)
    jj = lax.broadcasted_iota(jnp.int32, (CHUNK, CHUNK), 1)
    cs_c = _cumsum_rows(da_c, jnp.where(jj <= ii, 1.0, 0.0).astype(BF16))
    cs_r = _cumsum_lanes(da_r, jnp.where(ii <= jj, 1.0, 0.0).astype(BF16))
    tot = cs_c[CHUNK - 1:CHUNK, :]
    if reverse:
        a_c = da_c - cs_c
        a_r = da_r - cs_r
        off_scale = jnp.exp(tot + a_c)
        st_w = jnp.exp(-a_c) * dt_c
    else:
        a_c = cs_c
        a_r = cs_r
        off_scale = jnp.exp(cs_c)
        st_w = jnp.exp(tot - cs_c) * dt_c
    e = e_ref[...]
    off_x = _expand_heads(off_scale, e)
    stw_x = _expand_heads(st_w, e)
    dec_x = _expand_heads(jnp.broadcast_to(jnp.exp(tot), (HALO, SSM_HEADS)), e)[0:1, :]

    keep = (jj >= ii) if reverse else (ii >= jj)
    first_head = jj < SSM_HEAD_DIM

    for g in range(SSM_GROUPS):
        gs = slice(g * D_STATE, (g + 1) * D_STATE)
        ws = slice(g * GROUP_WIDTH, (g + 1) * GROUP_WIDTH)
        bg = bm_ref[0, :, gs]
        cg = cm_ref[0, :, gs]
        cb = lax.dot_general(cg, bg, NT_DIMS, preferred_element_type=F32)
        st = st_ref[:, ws]
        y_off = jnp.dot(cg, st.astype(BF16), preferred_element_type=F32) * off_x[:, ws]
        xw = (xs_ref[0, :, ws].astype(F32) * stw_x[:, ws]).astype(BF16)
        st_ref[:, ws] = st * dec_x[:, ws] + lax.dot_general(bg, xw, TN_DIMS, preferred_element_type=F32)
        for pr in range(HEADS_PER_GROUP // 2):
            h0 = g * HEADS_PER_GROUP + 2 * pr
            ps = slice(h0 * SSM_HEAD_DIM, (h0 + 2) * SSM_HEAD_DIM)
            x_p = xs_ref[0, :, ps]
            ys = []
            for h in (h0, h0 + 1):
                seg = a_c[:, h:h + 1] - a_r[h:h + 1, :]
                w = cb * jnp.exp(jnp.where(keep, seg, -jnp.inf)) * dt_r[h:h + 1, :]
                ys.append(jnp.dot(w.astype(BF16), x_p, preferred_element_type=F32))
            y_p = jnp.where(first_head, ys[0], ys[1]) + y_off[:, pr * 2 * SSM_HEAD_DIM:(pr + 1) * 2 * SSM_HEAD_DIM]
            if final:
                y_p = y_p + prev_ref[0, :, ps].astype(F32) + x_p.astype(F32) * dskip_ref[:, ps]
            y_ref[0, :, ps] = y_p.astype(BF16)


def _ssd(xs, bm, cm, dt, dtt, brow, bcol, arow, acol, e, prev=None, dskip=None, *, reverse):
    b, s, _ = xs.shape
    nc = s // CHUNK
    final = prev is not None
    ndt = 2 * SSM_HEADS

    def cidx(j):
        return nc - 1 - j if reverse else j

    def tok(width):
        return pl.BlockSpec((1, CHUNK, width), lambda i, j: (i, cidx(j), 0))

    in_specs = [tok(D_INNER), tok(BC_DIM), tok(BC_DIM), tok(ndt),
                pl.BlockSpec((1, ndt, CHUNK), lambda i, j: (i, 0, cidx(j))),
                _const_spec((2, SSM_HEADS)), _const_spec((2, SSM_HEADS, 1)),
                _const_spec((2, SSM_HEADS)), _const_spec((2, SSM_HEADS, 1)),
                _const_spec((SSM_HEADS, D_INNER))]
    args = [xs, bm, cm, dt, dtt, brow, bcol, arow, acol, e]
    if final:
        in_specs += [tok(D_INNER), _const_spec((1, D_INNER))]
        args += [prev, dskip]
    return pl.pallas_call(
        functools.partial(_ssd_kernel, reverse=reverse, final=final),
        grid=(b, nc),
        in_specs=in_specs,
        out_specs=tok(D_INNER),
        out_shape=jax.ShapeDtypeStruct((b, s, D_INNER), BF16),
        scratch_shapes=[pltpu.VMEM((D_STATE, D_INNER), F32)],
        compiler_params=_cparams(("parallel", "arbitrary")),
        name="ssd_bwd" if reverse else "ssd_fwd",
    )(*args)


def _ssm_out_kernel(x_ref, y_ref, z_ref, ng_ref, w_ref, gpost_ref, o_ref):
    y = y_ref[...].astype(F32) * _silu(z_ref[...].astype(F32))
    parts = []
    for g in range(SSM_GROUPS):
        yg = y[:, g * GROUP_WIDTH:(g + 1) * GROUP_WIDTH]
        parts.append(yg * lax.rsqrt(jnp.mean(yg * yg, axis=-1, keepdims=True) + EPS))
    yn = (jnp.concatenate(parts, axis=-1) * ng_ref[...]).astype(BF16)
    m = jnp.dot(yn, w_ref[...], preferred_element_type=F32)
    o_ref[...] = x_ref[...] + _rms(m, gpost_ref[...])


def _ssm_out(x2, y2, z2, ng, w, g_post):
    n = x2.shape[0]
    tm = _token_tile(n, 512)

    def row(width):
        return pl.BlockSpec((tm, width), lambda i: (i, 0))

    return pl.pallas_call(
        _ssm_out_kernel,
        grid=(n // tm,),
        in_specs=[row(D_MODEL), row(D_INNER), row(D_INNER), _const_spec((1, D_INNER)),
                  _const_spec((D_INNER, D_MODEL)), _const_spec((1, D_MODEL))],
        out_specs=row(D_MODEL),
        out_shape=jax.ShapeDtypeStruct((n, D_MODEL), F32),
        compiler_params=_cparams(("parallel",)),
        name="ssm_out",
    )(x2, y2, z2, ng, w, g_post)


def _qkv_kernel(x_ref, g_ref, w_ref, o_ref):
    u = _rms(x_ref[...], g_ref[...]).astype(BF16)
    o_ref[...] = jnp.dot(u, w_ref[...], preferred_element_type=F32).astype(BF16)


def _qkv(x2, g, w):
    n = x2.shape[0]
    tm = _token_tile(n, 512)
    return pl.pallas_call(
        _qkv_kernel,
        grid=(n // tm,),
        in_specs=[pl.BlockSpec((tm, D_MODEL), lambda i: (i, 0)), _const_spec((1, D_MODEL)),
                  _const_spec((D_MODEL, 3 * ATTN_DIM))],
        out_specs=pl.BlockSpec((tm, 3 * ATTN_DIM), lambda i: (i, 0)),
        out_shape=jax.ShapeDtypeStruct((n, 3 * ATTN_DIM), BF16),
        compiler_params=_cparams(("parallel",)),
        name="attn_qkv",
    )(x2, g, w)


def _attn_kernel(lam_ref, tbl_ref, bucket_ref, q_ref, k_ref, v_ref, g_ref, o_ref, slab_ref, s_ref, p_ref, vv_ref,
                 q2_ref, *, out_scale):
    tq = slab_ref.shape[0]
    s = k_ref.shape[1]
    nk = s // tq
    h = pl.program_id(0)

    @pl.when(pl.program_id(1) == 0)
    def _():
        bk = bucket_ref[...]
        acc = jnp.zeros(bk.shape, F32)
        for kk in range(N_BUCKETS):
            acc = jnp.where(bk == kk, tbl_ref[h, kk], acc)
        slab_ref[...] = acc

    lam = lam_ref[0]
    g = g_ref[...]
    vv_ref[...] = jnp.concatenate([v_ref[0], jnp.ones((s, ATTN_WIDTH), BF16)], axis=1)
    kchunk = 2 * tq if s % (2 * tq) == 0 else tq
    rblk = min(ATTN_ROW_BLOCK, tq)

    def q_tile(qi, carry):
        q0 = pl.multiple_of(qi * tq, tq)
        q = q_ref[0, pl.ds(q0, tq), :]
        lane = lax.broadcasted_iota(jnp.int32, q.shape, 1)
        zero = jnp.zeros_like(q)
        q2_ref[:tq, :] = jnp.where(lane < ATTN_HEAD_DIM, q, zero)
        q2_ref[tq:, :] = jnp.where(lane >= ATTN_HEAD_DIM, q, zero)
        mxs = []
        for r0 in range(0, 2 * tq, rblk):
            a0 = r0 % tq
            mrun = jnp.full((rblk, 128), -jnp.inf, F32)
            for k0 in range(0, s, kchunk):
                tiles = []
                for kc in range(k0 // tq, (k0 + kchunk) // tq):
                    d = jnp.clip(kc - qi, -SLAB_REACH, SLAB_REACH) + SLAB_REACH
                    tiles.append(slab_ref[a0:a0 + rblk, pl.ds(pl.multiple_of(d * tq, tq), tq)])
                bias = jnp.concatenate(tiles, axis=1) if len(tiles) > 1 else tiles[0]
                sc = lax.dot_general(q2_ref[r0:r0 + rblk, :], k_ref[0, k0:k0 + kchunk, :], NT_DIMS,
                                     preferred_element_type=F32) + bias
                s_ref[r0:r0 + rblk, k0:k0 + kchunk] = sc
                for j in range(kchunk // 128):
                    mrun = jnp.maximum(mrun, sc[:, j * 128:(j + 1) * 128])
            mxs.append(jnp.max(mrun, axis=-1, keepdims=True))
        mx = jnp.concatenate(mxs, axis=0)
        p_ref[...] = jnp.exp2(s_ref[...] - mx).astype(BF16)
        vv = vv_ref[...]
        acc1 = jnp.dot(p_ref[:tq, :], vv, preferred_element_type=F32)
        acc2 = jnp.dot(p_ref[tq:, :], vv, preferred_element_type=F32)
        o1 = acc1[:, :ATTN_WIDTH] / acc1[:, ATTN_WIDTH:ATTN_WIDTH + 1]
        o2 = acc2[:, :ATTN_WIDTH] / acc2[:, ATTN_WIDTH:ATTN_WIDTH + 1]
        o_ref[0, pl.ds(q0, tq), :] = (_rms(o1 - lam * o2, g) * out_scale).astype(BF16)
        return carry

    lax.fori_loop(0, s // tq, q_tile, 0, unroll=2)


def _attention(lam, tbl, bucket, qkv, subln, *, out_scale):
    b, s, _ = qkv.shape
    tq = bucket.shape[0]

    def head_block(col0):
        return pl.BlockSpec((1, s, ATTN_WIDTH), lambda h, i: (i, 0, col0 + h))

    return pl.pallas_call(
        functools.partial(_attn_kernel, out_scale=out_scale),
        grid=(ATTN_HEADS, b),
        in_specs=[pl.BlockSpec(memory_space=pltpu.SMEM), pl.BlockSpec(memory_space=pltpu.SMEM),
                  _const_spec(bucket.shape), head_block(0), head_block(ATTN_HEADS), head_block(2 * ATTN_HEADS),
                  _const_spec((1, ATTN_WIDTH))],
        out_specs=head_block(0),
        out_shape=jax.ShapeDtypeStruct((b, s, ATTN_DIM), BF16),
        scratch_shapes=[pltpu.VMEM(bucket.shape, F32), pltpu.VMEM((2 * tq, s), F32), pltpu.VMEM((2 * tq, s), BF16),
                        pltpu.VMEM((s, 2 * ATTN_WIDTH), BF16), pltpu.VMEM((2 * tq, ATTN_WIDTH), BF16)],
        compiler_params=_cparams(("arbitrary", "arbitrary")),
        name="diff_attn",
    )(lam, tbl, bucket, qkv, qkv, qkv, subln)


def _proj_out_kernel(x_ref, a_ref, w_ref, gpost_ref, o_ref):
    m = jnp.dot(a_ref[...], w_ref[...], preferred_element_type=F32)
    o_ref[...] = x_ref[...] + _rms(m, gpost_ref[...])


def _proj_out(x2, a2, w, g_post):
    n = x2.shape[0]
    k = a2.shape[1]
    tm = _token_tile(n, 512)
    return pl.pallas_call(
        _proj_out_kernel,
        grid=(n // tm,),
        in_specs=[pl.BlockSpec((tm, D_MODEL), lambda i: (i, 0)), pl.BlockSpec((tm, k), lambda i: (i, 0)),
                  _const_spec((k, D_MODEL)), _const_spec((1, D_MODEL))],
        out_specs=pl.BlockSpec((tm, D_MODEL), lambda i: (i, 0)),
        out_shape=jax.ShapeDtypeStruct((n, D_MODEL), F32),
        compiler_params=_cparams(("parallel",)),
        name="proj_out",
    )(x2, a2, w, g_post)


def _relative_bucket(rel):
    half = N_BUCKETS // 2
    max_exact = half // 2
    ret = jnp.where(rel > 0, half, 0)
    n = jnp.abs(rel)
    nf = jnp.maximum(n, 1).astype(F32)
    large = max_exact + (jnp.log(nf / max_exact) / math.log(MAX_DISTANCE / max_exact) * (half - max_exact)).astype(jnp.int32)
    large = jnp.minimum(large, half - 1)
    return ret + jnp.where(n < max_exact, n, large)


def _bucket_slab(tq):
    a = jnp.arange(tq)[:, None]
    c = jnp.arange((2 * SLAB_REACH + 1) * tq)[None, :]
    return _relative_bucket(c - a - SLAB_REACH * tq).astype(jnp.int32)


def _row(v):
    return v.reshape(1, -1).astype(F32)


def _mamba_layer(x, pre_g, post_g, w_in, conv_w, conv_b, dt_bias, a_log, d_skip, norm_g, w_out):
    b, s, _ = x.shape
    wz = w_in[:, :D_INNER].astype(BF16)
    wx = w_in[:, D_INNER:D_INNER + CONV_DIM].astype(BF16)
    wdt = w_in[:, D_INNER + CONV_DIM:].astype(BF16)
    z, xbc, dt, dtt = _ssm_in(x, _row(pre_g), wz, wx, wdt, wdt.T)
    xs, bm, cm = _conv(xbc, conv_w.astype(F32), _row(conv_b))
    a = -jnp.exp(a_log.astype(F32))
    brow = dt_bias.astype(F32)
    e = jnp.repeat(jnp.eye(SSM_HEADS, dtype=BF16), SSM_HEAD_DIM, axis=1)
    dskip = jnp.repeat(d_skip.astype(F32), SSM_HEAD_DIM).reshape(1, D_INNER)
    common = (xs, bm, cm, dt, dtt, brow, brow[:, :, None], a, a[:, :, None], e)
    y_fwd = _ssd(*common, reverse=False)
    y = _ssd(*common, y_fwd, dskip, reverse=True)
    out = _ssm_out(x.reshape(b * s, D_MODEL), y.reshape(b * s, D_INNER), z.reshape(b * s, D_INNER),
                   _row(norm_g), w_out.astype(BF16), _row(post_g))
    return out.reshape(b, s, D_MODEL)


def _attn_layer(x, pre_g, post_g, w_qkv, lam, subln_g, w_out, rel_bias, lambda_init):
    b, s, _ = x.shape
    scale = ATTN_HEAD_DIM ** -0.5 * LOG2E
    col_scale = jnp.concatenate([jnp.full((ATTN_DIM,), scale, F32), jnp.ones((2 * ATTN_DIM,), F32)])
    w = (w_qkv.astype(F32) * col_scale).astype(BF16)
    qkv = _qkv(x.reshape(b * s, D_MODEL), _row(pre_g), w).reshape(b, s, 3 * ATTN_DIM)
    lf = lam.astype(F32)
    lam_full = jnp.exp(jnp.sum(lf[0] * lf[1])) - jnp.exp(jnp.sum(lf[2] * lf[3])) + lambda_init
    tq = _token_tile(s, ATTN_Q_TILE)
    assert tq >= MAX_DISTANCE, "bias tiles two q tiles away must lie in the saturated buckets"
    tbl = rel_bias.astype(F32).T * LOG2E
    o = _attention(lam_full.reshape(1), tbl, _bucket_slab(tq), qkv, _row(subln_g), out_scale=1.0 - lambda_init)
    out = _proj_out(x.reshape(b * s, D_MODEL), o.reshape(b * s, ATTN_DIM), w_out.astype(BF16), _row(post_g))
    return out.reshape(b, s, D_MODEL)


def _trunk(x, norm_pre, norm_post, ffn_w_gate, ffn_w_up, ffn_w_down, ssm_w_in, ssm_conv_w, ssm_conv_b,
           ssm_dt_bias, ssm_a_log, ssm_d, ssm_norm, ssm_w_out, attn_w_qkv, attn_lambda, attn_subln,
           attn_w_out, rel_bias):
    b, s, _ = x.shape
    depth = norm_pre.shape[0]

    def ffn(x, i, k, slot):
        y = _ffn(x.reshape(b * s, D_MODEL), _row(norm_pre[i, slot]), ffn_w_gate[i, k].astype(BF16),
                 ffn_w_up[i, k].astype(BF16), ffn_w_down[i, k].astype(BF16), _row(norm_post[i, slot]))
        return y.reshape(b, s, D_MODEL)

    for i in range(depth):
        x = ffn(x, i, 0, 0)
        j = i // 2
        if i % 2 == 0:
            x = _mamba_layer(x, norm_pre[i, 1], norm_post[i, 1], ssm_w_in[j], ssm_conv_w[j], ssm_conv_b[j],
                             ssm_dt_bias[j], ssm_a_log[j], ssm_d[j], ssm_norm[j], ssm_w_out[j])
        else:
            lambda_init = 0.8 - 0.6 * math.exp(-0.3 * i)
            x = _attn_layer(x, norm_pre[i, 1], norm_post[i, 1], attn_w_qkv[j], attn_lambda[j], attn_subln[j],
                            attn_w_out[j], rel_bias, lambda_init)
        x = ffn(x, i, 1, 2)
    return x


def kernel(x_prompt, x_sample, norm_pre, norm_post, ffn_w_gate, ffn_w_up, ffn_w_down, ssm_w_in, ssm_conv_w, ssm_conv_b, ssm_dt_bias, ssm_a_log, ssm_d, ssm_norm, ssm_w_out, attn_w_qkv, attn_lambda, attn_subln, attn_w_out, rel_bias):
    params = (norm_pre, norm_post, ffn_w_gate, ffn_w_up, ffn_w_down, ssm_w_in, ssm_conv_w, ssm_conv_b,
              ssm_dt_bias, ssm_a_log, ssm_d, ssm_norm, ssm_w_out, attn_w_qkv, attn_lambda, attn_subln,
              attn_w_out, rel_bias)
    return (_trunk(x_prompt, *params), _trunk(x_sample, *params))
```

```python
import functools
import math

import jax
import jax.numpy as jnp
from jax import lax
from jax.experimental import pallas as pl
from jax.experimental.pallas import tpu as pltpu

F32 = jnp.float32
BF16 = jnp.bfloat16

EPS = 1e-6
D_MODEL = 1024
D_FF = 2816
FF_CHUNK = 256

D_INNER = 2048
SSM_HEAD_DIM = 64
SSM_HEADS = 32
SSM_GROUPS = 4
HEADS_PER_GROUP = SSM_HEADS // SSM_GROUPS
D_STATE = 128
D_CONV = 5
CHUNK = 128
BC_DIM = SSM_GROUPS * D_STATE
CONV_DIM = D_INNER + 2 * BC_DIM
GROUP_WIDTH = D_INNER // SSM_GROUPS

ATTN_HEADS = 8
ATTN_HEAD_DIM = 64
ATTN_WIDTH = 2 * ATTN_HEAD_DIM
ATTN_DIM = ATTN_HEADS * ATTN_WIDTH
N_BUCKETS = 32
MAX_DISTANCE = 128
ATTN_Q_TILE = 512
ATTN_ROW_BLOCK = 128
SLAB_REACH = 2
LOG2E = math.log2(math.e)

HALO = 16
CONV_ROWS = 128
CONV_WINDOW = 256
VMEM_LIMIT = 56 * 1024 * 1024

NT_DIMS = (((1,), (1,)), ((), ()))
TN_DIMS = (((0,), (0,)), ((), ()))


def _cparams(sem):
    return pltpu.CompilerParams(dimension_semantics=sem, vmem_limit_bytes=VMEM_LIMIT)


def _rms(x, g):
    return x * lax.rsqrt(jnp.mean(x * x, axis=-1, keepdims=True) + EPS) * g


def _silu(x):
    return x * (1.0 / (1.0 + jnp.exp(-x)))


def _const_spec(shape):
    nd = len(shape)
    return pl.BlockSpec(shape, lambda *_: (0,) * nd, pipeline_mode=pl.Buffered(1))


def _token_tile(n, pref):
    t = min(pref, n)
    while n % t:
        t //= 2
    return t


def _ffn_kernel(x_ref, gpre_ref, wg_ref, wu_ref, wd_ref, gpost_ref, o_ref, xn_ref, acc_ref):
    xn_ref[...] = _rms(x_ref[...], gpre_ref[...]).astype(BF16)
    for c in range(D_FF // FF_CHUNK):
        sl = slice(c * FF_CHUNK, (c + 1) * FF_CHUNK)
        xn = xn_ref[...]
        g = jnp.dot(xn, wg_ref[:, sl], preferred_element_type=F32)
        u = jnp.dot(xn, wu_ref[:, sl], preferred_element_type=F32)
        h = (_silu(g) * u).astype(BF16)
        d = jnp.dot(h, wd_ref[sl, :], preferred_element_type=F32)
        if c == 0:
            acc_ref[...] = d
        else:
            acc_ref[...] += d
    o_ref[...] = x_ref[...] + 0.5 * _rms(acc_ref[...], gpost_ref[...])


def _ffn(x2, g_pre, wg, wu, wd, g_post):
    n = x2.shape[0]
    tm = _token_tile(n, 512)
    row = pl.BlockSpec((tm, D_MODEL), lambda i: (i, 0))
    return pl.pallas_call(
        _ffn_kernel,
        grid=(n // tm,),
        in_specs=[row, _const_spec((1, D_MODEL)), _const_spec((D_MODEL, D_FF)), _const_spec((D_MODEL, D_FF)),
                  _const_spec((D_FF, D_MODEL)), _const_spec((1, D_MODEL))],
        out_specs=row,
        out_shape=jax.ShapeDtypeStruct((n, D_MODEL), F32),
        scratch_shapes=[pltpu.VMEM((tm, D_MODEL), BF16), pltpu.VMEM((tm, D_MODEL), F32)],
        compiler_params=_cparams(("parallel",)),
        name="ffn",
    )(x2, g_pre, wg, wu, wd, g_post)


def _ssm_in_kernel(x_ref, g_ref, wz_ref, wx_ref, wdt_ref, wdtt_ref, z_ref, xbc_ref, dt_ref, dtt_ref):
    u = _rms(x_ref[0], g_ref[...]).astype(BF16)
    z_ref[0] = jnp.dot(u, wz_ref[...], preferred_element_type=F32).astype(BF16)
    xbc_ref[0] = jnp.dot(u, wx_ref[...], preferred_element_type=F32).astype(BF16)
    dt_ref[0] = jnp.dot(u, wdt_ref[...], preferred_element_type=F32)
    dtt_ref[0] = lax.dot_general(wdtt_ref[...], u, NT_DIMS, preferred_element_type=F32)


def _ssm_in(x, g, wz, wx, wdt, wdtt):
    b, s, _ = x.shape
    tm = _token_tile(s, 512)
    ndt = 2 * SSM_HEADS

    def tok(width):
        return pl.BlockSpec((1, tm, width), lambda i, j: (i, j, 0))

    return pl.pallas_call(
        _ssm_in_kernel,
        grid=(b, s // tm),
        in_specs=[tok(D_MODEL), _const_spec((1, D_MODEL)), _const_spec((D_MODEL, D_INNER)),
                  _const_spec((D_MODEL, CONV_DIM)), _const_spec((D_MODEL, ndt)), _const_spec((ndt, D_MODEL))],
        out_specs=[tok(D_INNER), tok(CONV_DIM), tok(ndt), pl.BlockSpec((1, ndt, tm), lambda i, j: (i, 0, j))],
        out_shape=[jax.ShapeDtypeStruct((b, s, D_INNER), BF16), jax.ShapeDtypeStruct((b, s, CONV_DIM), BF16),
                   jax.ShapeDtypeStruct((b, s, ndt), F32), jax.ShapeDtypeStruct((b, ndt, s), F32)],
        compiler_params=_cparams(("parallel", "parallel")),
        name="ssm_in",
    )(x, g, wz, wx, wdt, wdtt)


def _conv_kernel(prev_ref, main_ref, next_ref, w_ref, b_ref, xs_ref, bm_ref, cm_ref, ext_ref):
    j = pl.program_id(1)
    tc = main_ref.shape[1]
    zero = jnp.zeros((HALO, CONV_DIM), BF16)
    ext_ref[pl.ds(0, HALO), :] = jnp.where(j > 0, prev_ref[0], zero)
    ext_ref[pl.ds(HALO, tc), :] = main_ref[0]
    ext_ref[pl.ds(HALO + tc, HALO), :] = jnp.where(j < pl.num_programs(1) - 1, next_ref[0], zero)
    pad = D_CONV // 2
    rows = min(CONV_ROWS, tc)
    win_len = min(CONV_WINDOW, tc + 2 * HALO)
    taps = [k for k in range(D_CONV) if k != pad]
    for r0 in range(0, tc, rows):
        w0 = min(r0, tc + 2 * HALO - win_len)
        out_row = lax.broadcasted_iota(jnp.int32, (rows, win_len), 0) + (HALO + r0 - w0 - pad)
        win_row = lax.broadcasted_iota(jnp.int32, (rows, win_len), 1)
        sel = jnp.concatenate([jnp.where(win_row == out_row + k, 1.0, 0.0).astype(BF16) for k in taps], axis=0)
        shifted = jnp.dot(sel, ext_ref[pl.ds(w0, win_len), :], preferred_element_type=F32)
        acc = b_ref[...] + w_ref[pad:pad + 1, :] * ext_ref[pl.ds(HALO + r0, rows), :].astype(F32)
        for n, k in enumerate(taps):
            acc = acc + w_ref[k:k + 1, :] * shifted[n * rows:(n + 1) * rows]
        y = _silu(acc).astype(BF16)
        xs_ref[0, pl.ds(r0, rows), :] = y[:, :D_INNER]
        bm_ref[0, pl.ds(r0, rows), :] = y[:, D_INNER:D_INNER + BC_DIM]
        cm_ref[0, pl.ds(r0, rows), :] = y[:, D_INNER + BC_DIM:]


def _conv(xbc, w, bias):
    b, s, _ = xbc.shape
    tc = _token_tile(s, 256)
    nb = tc // HALO
    last = s // HALO - 1

    def tok(width):
        return pl.BlockSpec((1, tc, width), lambda i, j: (i, j, 0))

    return pl.pallas_call(
        _conv_kernel,
        grid=(b, s // tc),
        in_specs=[pl.BlockSpec((1, HALO, CONV_DIM), lambda i, j: (i, jnp.maximum(j * nb - 1, 0), 0)),
                  tok(CONV_DIM),
                  pl.BlockSpec((1, HALO, CONV_DIM), lambda i, j: (i, jnp.minimum((j + 1) * nb, last), 0)),
                  _const_spec((D_CONV, CONV_DIM)), _const_spec((1, CONV_DIM))],
        out_specs=[tok(D_INNER), tok(BC_DIM), tok(BC_DIM)],
        out_shape=[jax.ShapeDtypeStruct((b, s, D_INNER), BF16), jax.ShapeDtypeStruct((b, s, BC_DIM), BF16),
                   jax.ShapeDtypeStruct((b, s, BC_DIM), BF16)],
        scratch_shapes=[pltpu.VMEM((tc + 2 * HALO, CONV_DIM), BF16)],
        compiler_params=_cparams(("parallel", "parallel")),
        name="ssm_conv",
    )(xbc, xbc, xbc, w, bias)


def _split3(v):
    hi = v.astype(BF16)
    r1 = v - hi.astype(F32)
    mid = r1.astype(BF16)
    lo = (r1 - mid.astype(F32)).astype(BF16)
    return hi, mid, lo


def _cumsum_rows(x, lower):
    return sum(jnp.dot(lower, p, preferred_element_type=F32) for p in _split3(x))


def _cumsum_lanes(x, upper):
    return sum(jnp.dot(p, upper, preferred_element_type=F32) for p in _split3(x))


def _expand_heads(v, e):
    hi = v.astype(BF16)
    lo = (v - hi.astype(F32)).astype(BF16)
    return jnp.dot(hi, e, preferred_element_type=F32) + jnp.dot(lo, e, preferred_element_type=F32)


def _ssd_kernel(*refs, reverse, final):
    if final:
        (xs_ref, bm_ref, cm_ref, dt_ref, dtt_ref, brow_ref, bcol_ref, arow_ref, acol_ref, e_ref,
         prev_ref, dskip_ref, y_ref, st_ref) = refs
    else:
        (xs_ref, bm_ref, cm_ref, dt_ref, dtt_ref, brow_ref, bcol_ref, arow_ref, acol_ref, e_ref,
         y_ref, st_ref) = refs
    d = 1 if reverse else 0
    hs = slice(d * SSM_HEADS, (d + 1) * SSM_HEADS)

    @pl.when(pl.program_id(1) == 0)
    def _():
        st_ref[...] = jnp.zeros_like(st_ref)

    dt_c = jax.nn.softplus(dt_ref[0, :, hs] + brow_ref[d:d + 1, :])
    dt_r = jax.nn.softplus(dtt_ref[0, hs, :] + bcol_ref[d])
    da_c = dt_c * arow_ref[d:d + 1, :]
    da_r = dt_r * acol_ref[d]
    ii = lax.broadcasted_iota(jnp.int32, (CHUNK, CHUNK), 0)
    jj = lax.broadcasted_iota(jnp.int32, (CHUNK, CHUNK), 1)
    cs_c = _cumsum_rows(da_c, jnp.where(jj <= ii, 1.0, 0.0).astype(BF16))
    cs_r = _cumsum_lanes(da_r, jnp.where(ii <= jj, 1.0, 0.0).astype(BF16))
    tot = cs_c[CHUNK - 1:CHUNK, :]
    if reverse:
        a_c = da_c - cs_c
        a_r = da_r - cs_r
        off_scale = jnp.exp(tot + a_c)
        st_w = jnp.exp(-a_c) * dt_c
    else:
        a_c = cs_c
        a_r = cs_r
        off_scale = jnp.exp(cs_c)
        st_w = jnp.exp(tot - cs_c) * dt_c
    e = e_ref[...]
    off_x = _expand_heads(off_scale, e)
    stw_x = _expand_heads(st_w, e)
    dec_x = _expand_heads(jnp.broadcast_to(jnp.exp(tot), (HALO, SSM_HEADS)), e)[0:1, :]

    keep = (jj >= ii) if reverse else (ii >= jj)
    first_head = jj < SSM_HEAD_DIM

    for g in range(SSM_GROUPS):
        gs = slice(g * D_STATE, (g + 1) * D_STATE)
        ws = slice(g * GROUP_WIDTH, (g + 1) * GROUP_WIDTH)
        bg = bm_ref[0, :, gs]
        cg = cm_ref[0, :, gs]
        cb = lax.dot_general(cg, bg, NT_DIMS, preferred_element_type=F32)
        st = st_ref[:, ws]
        y_off = jnp.dot(cg, st.astype(BF16), preferred_element_type=F32) * off_x[:, ws]
        xw = (xs_ref[0, :, ws].astype(F32) * stw_x[:, ws]).astype(BF16)
        st_ref[:, ws] = st * dec_x[:, ws] + lax.dot_general(bg, xw, TN_DIMS, preferred_element_type=F32)
        for pr in range(HEADS_PER_GROUP // 2):
            h0 = g * HEADS_PER_GROUP + 2 * pr
            ps = slice(h0 * SSM_HEAD_DIM, (h0 + 2) * SSM_HEAD_DIM)
            x_p = xs_ref[0, :, ps]
            ys = []
            for h in (h0, h0 + 1):
                seg = a_c[:, h:h + 1] - a_r[h:h + 1, :]
                w = cb * jnp.exp(jnp.where(keep, seg, -jnp.inf)) * dt_r[h:h + 1, :]
                ys.append(jnp.dot(w.astype(BF16), x_p, preferred_element_type=F32))
            y_p = jnp.where(first_head, ys[0], ys[1]) + y_off[:, pr * 2 * SSM_HEAD_DIM:(pr + 1) * 2 * SSM_HEAD_DIM]
            if final:
                y_p = y_p + prev_ref[0, :, ps].astype(F32) + x_p.astype(F32) * dskip_ref[:, ps]
            y_ref[0, :, ps] = y_p.astype(BF16)


def _ssd(xs, bm, cm, dt, dtt, brow, bcol, arow, acol, e, prev=None, dskip=None, *, reverse):
    b, s, _ = xs.shape
    nc = s // CHUNK
    final = prev is not None
    ndt = 2 * SSM_HEADS

    def cidx(j):
        return nc - 1 - j if reverse else j

    def tok(width):
        return pl.BlockSpec((1, CHUNK, width), lambda i, j: (i, cidx(j), 0))

    in_specs = [tok(D_INNER), tok(BC_DIM), tok(BC_DIM), tok(ndt),
                pl.BlockSpec((1, ndt, CHUNK), lambda i, j: (i, 0, cidx(j))),
                _const_spec((2, SSM_HEADS)), _const_spec((2, SSM_HEADS, 1)),
                _const_spec((2, SSM_HEADS)), _const_spec((2, SSM_HEADS, 1)),
                _const_spec((SSM_HEADS, D_INNER))]
    args = [xs, bm, cm, dt, dtt, brow, bcol, arow, acol, e]
    if final:
        in_specs += [tok(D_INNER), _const_spec((1, D_INNER))]
        args += [prev, dskip]
    return pl.pallas_call(
        functools.partial(_ssd_kernel, reverse=reverse, final=final),
        grid=(b, nc),
        in_specs=in_specs,
        out_specs=tok(D_INNER),
        out_shape=jax.ShapeDtypeStruct((b, s, D_INNER), BF16),
        scratch_shapes=[pltpu.VMEM((D_STATE, D_INNER), F32)],
        compiler_params=_cparams(("parallel", "arbitrary")),
        name="ssd_bwd" if reverse else "ssd_fwd",
    )(*args)


def _ssm_out_kernel(x_ref, y_ref, z_ref, ng_ref, w_ref, gpost_ref, o_ref):
    y = y_ref[...].astype(F32) * _silu(z_ref[...].astype(F32))
    parts = []
    for g in range(SSM_GROUPS):
        yg = y[:, g * GROUP_WIDTH:(g + 1) * GROUP_WIDTH]
        parts.append(yg * lax.rsqrt(jnp.mean(yg * yg, axis=-1, keepdims=True) + EPS))
    yn = (jnp.concatenate(parts, axis=-1) * ng_ref[...]).astype(BF16)
    m = jnp.dot(yn, w_ref[...], preferred_element_type=F32)
    o_ref[...] = x_ref[...] + _rms(m, gpost_ref[...])


def _ssm_out(x2, y2, z2, ng, w, g_post):
    n = x2.shape[0]
    tm = _token_tile(n, 512)

    def row(width):
        return pl.BlockSpec((tm, width), lambda i: (i, 0))

    return pl.pallas_call(
        _ssm_out_kernel,
        grid=(n // tm,),
        in_specs=[row(D_MODEL), row(D_INNER), row(D_INNER), _const_spec((1, D_INNER)),
                  _const_spec((D_INNER, D_MODEL)), _const_spec((1, D_MODEL))],
        out_specs=row(D_MODEL),
        out_shape=jax.ShapeDtypeStruct((n, D_MODEL), F32),
        compiler_params=_cparams(("parallel",)),
        name="ssm_out",
    )(x2, y2, z2, ng, w, g_post)


def _qkv_kernel(x_ref, g_ref, w_ref, o_ref):
    u = _rms(x_ref[...], g_ref[...]).astype(BF16)
    o_ref[...] = jnp.dot(u, w_ref[...], preferred_element_type=F32).astype(BF16)


def _qkv(x2, g, w):
    n = x2.shape[0]
    tm = _token_tile(n, 512)
    return pl.pallas_call(
        _qkv_kernel,
        grid=(n // tm,),
        in_specs=[pl.BlockSpec((tm, D_MODEL), lambda i: (i, 0)), _const_spec((1, D_MODEL)),
                  _const_spec((D_MODEL, 3 * ATTN_DIM))],
        out_specs=pl.BlockSpec((tm, 3 * ATTN_DIM), lambda i: (i, 0)),
        out_shape=jax.ShapeDtypeStruct((n, 3 * ATTN_DIM), BF16),
        compiler_params=_cparams(("parallel",)),
        name="attn_qkv",
    )(x2, g, w)


def _attn_kernel(lam_ref, tbl_ref, bucket_ref, q_ref, k_ref, v_ref, g_ref, o_ref, slab_ref, s_ref, p_ref, vv_ref,
                 q2_ref, *, out_scale):
    tq = slab_ref.shape[0]
    s = k_ref.shape[1]
    nk = s // tq
    h = pl.program_id(0)

    @pl.when(pl.program_id(1) == 0)
    def _():
        bk = bucket_ref[...]
        acc = jnp.zeros(bk.shape, F32)
        for kk in range(N_BUCKETS):
            acc = jnp.where(bk == kk, tbl_ref[h, kk], acc)
        slab_ref[...] = acc

    lam = lam_ref[0]
    g = g_ref[...]
    vv_ref[...] = jnp.concatenate([v_ref[0], jnp.ones((s, ATTN_WIDTH), BF16)], axis=1)
    kchunk = 2 * tq if s % (2 * tq) == 0 else tq
    rblk = min(ATTN_ROW_BLOCK, tq)

    def q_tile(qi, carry):
        q0 = pl.multiple_of(qi * tq, tq)
        q = q_ref[0, pl.ds(q0, tq), :]
        lane = lax.broadcasted_iota(jnp.int32, q.shape, 1)
        zero = jnp.zeros_like(q)
        q2_ref[:tq, :] = jnp.where(lane < ATTN_HEAD_DIM, q, zero)
        q2_ref[tq:, :] = jnp.where(lane >= ATTN_HEAD_DIM, q, zero)
        mxs = []
        for r0 in range(0, 2 * tq, rblk):
            a0 = r0 % tq
            mrun = jnp.full((rblk, 128), -jnp.inf, F32)
            for k0 in range(0, s, kchunk):
                tiles = []
                for kc in range(k0 // tq, (k0 + kchunk) // tq):
                    d = jnp.clip(kc - qi, -SLAB_REACH, SLAB_REACH) + SLAB_REACH
                    tiles.append(slab_ref[a0:a0 + rblk, pl.ds(pl.multiple_of(d * tq, tq), tq)])
                bias = jnp.concatenate(tiles, axis=1) if len(tiles) > 1 else tiles[0]
                sc = lax.dot_general(q2_ref[r0:r0 + rblk, :], k_ref[0, k0:k0 + kchunk, :], NT_DIMS,
                                     preferred_element_type=F32) + bias
                s_ref[r0:r0 + rblk, k0:k0 + kchunk] = sc
                for j in range(kchunk // 128):
                    mrun = jnp.maximum(mrun, sc[:, j * 128:(j + 1) * 128])
            mxs.append(jnp.max(mrun, axis=-1, keepdims=True))
        mx = jnp.concatenate(mxs, axis=0)
        p_ref[...] = jnp.exp2(s_ref[...] - mx).astype(BF16)
        vv = vv_ref[...]
        acc1 = jnp.dot(p_ref[:tq, :], vv, preferred_element_type=F32)
        acc2 = jnp.dot(p_ref[tq:, :], vv, preferred_element_type=F32)
        o1 = acc1[:, :ATTN_WIDTH] / acc1[:, ATTN_WIDTH:ATTN_WIDTH + 1]
        o2 = acc2[:, :ATTN_WIDTH] / acc2[:, ATTN_WIDTH:ATTN_WIDTH + 1]
        o_ref[0, pl.ds(q0, tq), :] = (_rms(o1 - lam * o2, g) * out_scale).astype(BF16)
        return carry

    lax.fori_loop(0, s // tq, q_tile, 0, unroll=2)


def _attention(lam, tbl, bucket, qkv, subln, *, out_scale):
    b, s, _ = qkv.shape
    tq = bucket.shape[0]

    def head_block(col0):
        return pl.BlockSpec((1, s, ATTN_WIDTH), lambda h, i: (i, 0, col0 + h))

    return pl.pallas_call(
        functools.partial(_attn_kernel, out_scale=out_scale),
        grid=(ATTN_HEADS, b),
        in_specs=[pl.BlockSpec(memory_space=pltpu.SMEM), pl.BlockSpec(memory_space=pltpu.SMEM),
                  _const_spec(bucket.shape), head_block(0), head_block(ATTN_HEADS), head_block(2 * ATTN_HEADS),
                  _const_spec((1, ATTN_WIDTH))],
        out_specs=head_block(0),
        out_shape=jax.ShapeDtypeStruct((b, s, ATTN_DIM), BF16),
        scratch_shapes=[pltpu.VMEM(bucket.shape, F32), pltpu.VMEM((2 * tq, s), F32), pltpu.VMEM((2 * tq, s), BF16),
                        pltpu.VMEM((s, 2 * ATTN_WIDTH), BF16), pltpu.VMEM((2 * tq, ATTN_WIDTH), BF16)],
        compiler_params=_cparams(("arbitrary", "arbitrary")),
        name="diff_attn",
    )(lam, tbl, bucket, qkv, qkv, qkv, subln)


def _proj_out_kernel(x_ref, a_ref, w_ref, gpost_ref, o_ref):
    m = jnp.dot(a_ref[...], w_ref[...], preferred_element_type=F32)
    o_ref[...] = x_ref[...] + _rms(m, gpost_ref[...])


def _proj_out(x2, a2, w, g_post):
    n = x2.shape[0]
    k = a2.shape[1]
    tm = _token_tile(n, 512)
    return pl.pallas_call(
        _proj_out_kernel,
        grid=(n // tm,),
        in_specs=[pl.BlockSpec((tm, D_MODEL), lambda i: (i, 0)), pl.BlockSpec((tm, k), lambda i: (i, 0)),
                  _const_spec((k, D_MODEL)), _const_spec((1, D_MODEL))],
        out_specs=pl.BlockSpec((tm, D_MODEL), lambda i: (i, 0)),
        out_shape=jax.ShapeDtypeStruct((n, D_MODEL), F32),
        compiler_params=_cparams(("parallel",)),
        name="proj_out",
    )(x2, a2, w, g_post)


def _relative_bucket(rel):
    half = N_BUCKETS // 2
    max_exact = half // 2
    ret = jnp.where(rel > 0, half, 0)
    n = jnp.abs(rel)
    nf = jnp.maximum(n, 1).astype(F32)
    large = max_exact + (jnp.log(nf / max_exact) / math.log(MAX_DISTANCE / max_exact) * (half - max_exact)).astype(jnp.int32)
    large = jnp.minimum(large, half - 1)
    return ret + jnp.where(n < max_exact, n, large)


def _bucket_slab(tq):
    a = jnp.arange(tq)[:, None]
    c = jnp.arange((2 * SLAB_REACH + 1) * tq)[None, :]
    return _relative_bucket(c - a - SLAB_REACH * tq).astype(jnp.int32)


def _row(v):
    return v.reshape(1, -1).astype(F32)


def _mamba_layer(x, pre_g, post_g, w_in, conv_w, conv_b, dt_bias, a_log, d_skip, norm_g, w_out):
    b, s, _ = x.shape
    wz = w_in[:, :D_INNER].astype(BF16)
    wx = w_in[:, D_INNER:D_INNER + CONV_DIM].astype(BF16)
    wdt = w_in[:, D_INNER + CONV_DIM:].astype(BF16)
    z, xbc, dt, dtt = _ssm_in(x, _row(pre_g), wz, wx, wdt, wdt.T)
    xs, bm, cm = _conv(xbc, conv_w.astype(F32), _row(conv_b))
    a = -jnp.exp(a_log.astype(F32))
    brow = dt_bias.astype(F32)
    e = jnp.repeat(jnp.eye(SSM_HEADS, dtype=BF16), SSM_HEAD_DIM, axis=1)
    dskip = jnp.repeat(d_skip.astype(F32), SSM_HEAD_DIM).reshape(1, D_INNER)
    common = (xs, bm, cm, dt, dtt, brow, brow[:, :, None], a, a[:, :, None], e)
    y_fwd = _ssd(*common, reverse=False)
    y = _ssd(*common, y_fwd, dskip, reverse=True)
    out = _ssm_out(x.reshape(b * s, D_MODEL), y.reshape(b * s, D_INNER), z.reshape(b * s, D_INNER),
                   _row(norm_g), w_out.astype(BF16), _row(post_g))
    return out.reshape(b, s, D_MODEL)


def _attn_layer(x, pre_g, post_g, w_qkv, lam, subln_g, w_out, rel_bias, lambda_init):
    b, s, _ = x.shape
    scale = ATTN_HEAD_DIM ** -0.5 * LOG2E
    col_scale = jnp.concatenate([jnp.full((ATTN_DIM,), scale, F32), jnp.ones((2 * ATTN_DIM,), F32)])
    w = (w_qkv.astype(F32) * col_scale).astype(BF16)
    qkv = _qkv(x.reshape(b * s, D_MODEL), _row(pre_g), w).reshape(b, s, 3 * ATTN_DIM)
    lf = lam.astype(F32)
    lam_full = jnp.exp(jnp.sum(lf[0] * lf[1])) - jnp.exp(jnp.sum(lf[2] * lf[3])) + lambda_init
    tq = _token_tile(s, ATTN_Q_TILE)
    assert tq >= MAX_DISTANCE, "bias tiles two q tiles away must lie in the saturated buckets"
    tbl = rel_bias.astype(F32).T * LOG2E
    o = _attention(lam_full.reshape(1), tbl, _bucket_slab(tq), qkv, _row(subln_g), out_scale=1.0 - lambda_init)
    out = _proj_out(x.reshape(b * s, D_MODEL), o.reshape(b * s, ATTN_DIM), w_out.astype(BF16), _row(post_g))
    return out.reshape(b, s, D_MODEL)


def _trunk(x, norm_pre, norm_post, ffn_w_gate, ffn_w_up, ffn_w_down, ssm_w_in, ssm_conv_w, ssm_conv_b,
           ssm_dt_bias, ssm_a_log, ssm_d, ssm_norm, ssm_w_out, attn_w_qkv, attn_lambda, attn_subln,
           attn_w_out, rel_bias):
    b, s, _ = x.shape
    depth = norm_pre.shape[0]

    def ffn(x, i, k, slot):
        y = _ffn(x.reshape(b * s, D_MODEL), _row(norm_pre[i, slot]), ffn_w_gate[i, k].astype(BF16),
                 ffn_w_up[i, k].astype(BF16), ffn_w_down[i, k].astype(BF16), _row(norm_post[i, slot]))
        return y.reshape(b, s, D_MODEL)

    for i in range(depth):
        x = ffn(x, i, 0, 0)
        j = i // 2
        if i % 2 == 0:
            x = _mamba_layer(x, norm_pre[i, 1], norm_post[i, 1], ssm_w_in[j], ssm_conv_w[j], ssm_conv_b[j],
                             ssm_dt_bias[j], ssm_a_log[j], ssm_d[j], ssm_norm[j], ssm_w_out[j])
        else:
            lambda_init = 0.8 - 0.6 * math.exp(-0.3 * i)
            x = _attn_layer(x, norm_pre[i, 1], norm_post[i, 1], attn_w_qkv[j], attn_lambda[j], attn_subln[j],
                            attn_w_out[j], rel_bias, lambda_init)
        x = ffn(x, i, 1, 2)
    return x


def kernel(x_prompt, x_sample, norm_pre, norm_post, ffn_w_gate, ffn_w_up, ffn_w_down, ssm_w_in, ssm_conv_w, ssm_conv_b, ssm_dt_bias, ssm_a_log, ssm_d, ssm_norm, ssm_w_out, attn_w_qkv, attn_lambda, attn_subln, attn_w_out, rel_bias):
    params = (norm_pre, norm_post, ffn_w_gate, ffn_w_up, ffn_w_down, ssm_w_in, ssm_conv_w, ssm_conv_b,
              ssm_dt_bias, ssm_a_log, ssm_d, ssm_norm, ssm_w_out, attn_w_qkv, attn_lambda, attn_subln,
              attn_w_out, rel_bias)
    return (_trunk(x_prompt, *params), _trunk(x_sample, *params))
```

```python
import functools
import math

import jax
import jax.numpy as jnp
from jax import lax
from jax.experimental import pallas as pl
from jax.experimental.pallas import tpu as pltpu

F32 = jnp.float32
BF16 = jnp.bfloat16

EPS = 1e-6
D_MODEL = 1024
D_FF = 2816
FF_CHUNK = 256

D_INNER = 2048
SSM_HEAD_DIM = 64
SSM_HEADS = 32
SSM_GROUPS = 4
HEADS_PER_GROUP = SSM_HEADS // SSM_GROUPS
D_STATE = 128
D_CONV = 5
CHUNK = 128
SSD_CHUNKS_PER_STEP = 4
BC_DIM = SSM_GROUPS * D_STATE
CONV_DIM = D_INNER + 2 * BC_DIM
GROUP_WIDTH = D_INNER // SSM_GROUPS

ATTN_HEADS = 8
ATTN_HEAD_DIM = 64
ATTN_WIDTH = 2 * ATTN_HEAD_DIM
ATTN_DIM = ATTN_HEADS * ATTN_WIDTH
N_BUCKETS = 32
MAX_DISTANCE = 128
ATTN_Q_TILE = 512
ATTN_ROW_BLOCK = 128
SLAB_REACH = 2
LOG2E = math.log2(math.e)

HALO = 16
CONV_ROWS = 128
CONV_WINDOW = 256
VMEM_LIMIT = 56 * 1024 * 1024

NT_DIMS = (((1,), (1,)), ((), ()))
TN_DIMS = (((0,), (0,)), ((), ()))


def _cparams(sem):
    return pltpu.CompilerParams(dimension_semantics=sem, vmem_limit_bytes=VMEM_LIMIT)


def _rms(x, g):
    return x * lax.rsqrt(jnp.mean(x * x, axis=-1, keepdims=True) + EPS) * g


def _silu(x):
    return x * (1.0 / (1.0 + jnp.exp(-x)))


def _const_spec(shape):
    nd = len(shape)
    return pl.BlockSpec(shape, lambda *_: (0,) * nd, pipeline_mode=pl.Buffered(1))


def _token_tile(n, pref):
    t = min(pref, n)
    while n % t:
        t //= 2
    return t


def _ffn_kernel(x_ref, gpre_ref, wg_ref, wu_ref, wd_ref, gpost_ref, o_ref, xn_ref, acc_ref):
    xn_ref[...] = _rms(x_ref[...], gpre_ref[...]).astype(BF16)
    for c in range(D_FF // FF_CHUNK):
        sl = slice(c * FF_CHUNK, (c + 1) * FF_CHUNK)
        xn = xn_ref[...]
        g = jnp.dot(xn, wg_ref[:, sl], preferred_element_type=F32)
        u = jnp.dot(xn, wu_ref[:, sl], preferred_element_type=F32)
        h = (_silu(g) * u).astype(BF16)
        d = jnp.dot(h, wd_ref[sl, :], preferred_element_type=F32)
        if c == 0:
            acc_ref[...] = d
        else:
            acc_ref[...] += d
    o_ref[...] = x_ref[...] + 0.5 * _rms(acc_ref[...], gpost_ref[...])


def _ffn(x2, g_pre, wg, wu, wd, g_post):
    n = x2.shape[0]
    tm = _token_tile(n, 512)
    row = pl.BlockSpec((tm, D_MODEL), lambda i: (i, 0))
    return pl.pallas_call(
        _ffn_kernel,
        grid=(n // tm,),
        in_specs=[row, _const_spec((1, D_MODEL)), _const_spec((D_MODEL, D_FF)), _const_spec((D_MODEL, D_FF)),
                  _const_spec((D_FF, D_MODEL)), _const_spec((1, D_MODEL))],
        out_specs=row,
        out_shape=jax.ShapeDtypeStruct((n, D_MODEL), F32),
        scratch_shapes=[pltpu.VMEM((tm, D_MODEL), BF16), pltpu.VMEM((tm, D_MODEL), F32)],
        compiler_params=_cparams(("parallel",)),
        name="ffn",
    )(x2, g_pre, wg, wu, wd, g_post)


def _ssm_in_kernel(x_ref, g_ref, wz_ref, wx_ref, wdt_ref, wdtt_ref, z_ref, xbc_ref, dt_ref, dtt_ref):
    u = _rms(x_ref[0], g_ref[...]).astype(BF16)
    z_ref[0] = jnp.dot(u, wz_ref[...], preferred_element_type=F32).astype(BF16)
    xbc_ref[0] = jnp.dot(u, wx_ref[...], preferred_element_type=F32).astype(BF16)
    dt_ref[0] = jnp.dot(u, wdt_ref[...], preferred_element_type=F32)
    dtt_ref[0] = lax.dot_general(wdtt_ref[...], u, NT_DIMS, preferred_element_type=F32)


def _ssm_in(x, g, wz, wx, wdt, wdtt):
    b, s, _ = x.shape
    tm = _token_tile(s, 512)
    ndt = 2 * SSM_HEADS

    def tok(width):
        return pl.BlockSpec((1, tm, width), lambda i, j: (i, j, 0))

    return pl.pallas_call(
        _ssm_in_kernel,
        grid=(b, s // tm),
        in_specs=[tok(D_MODEL), _const_spec((1, D_MODEL)), _const_spec((D_MODEL, D_INNER)),
                  _const_spec((D_MODEL, CONV_DIM)), _const_spec((D_MODEL, ndt)), _const_spec((ndt, D_MODEL))],
        out_specs=[tok(D_INNER), tok(CONV_DIM), tok(ndt), pl.BlockSpec((1, ndt, tm), lambda i, j: (i, 0, j))],
        out_shape=[jax.ShapeDtypeStruct((b, s, D_INNER), BF16), jax.ShapeDtypeStruct((b, s, CONV_DIM), BF16),
                   jax.ShapeDtypeStruct((b, s, ndt), F32), jax.ShapeDtypeStruct((b, ndt, s), F32)],
        compiler_params=_cparams(("parallel", "parallel")),
        name="ssm_in",
    )(x, g, wz, wx, wdt, wdtt)


def _conv_kernel(prev_ref, main_ref, next_ref, w_ref, b_ref, xs_ref, bm_ref, cm_ref, ext_ref):
    j = pl.program_id(1)
    tc = main_ref.shape[1]
    zero = jnp.zeros((HALO, CONV_DIM), BF16)
    ext_ref[pl.ds(0, HALO), :] = jnp.where(j > 0, prev_ref[0], zero)
    ext_ref[pl.ds(HALO, tc), :] = main_ref[0]
    ext_ref[pl.ds(HALO + tc, HALO), :] = jnp.where(j < pl.num_programs(1) - 1, next_ref[0], zero)
    pad = D_CONV // 2
    rows = min(CONV_ROWS, tc)
    win_len = min(CONV_WINDOW, tc + 2 * HALO)
    taps = [k for k in range(D_CONV) if k != pad]
    for r0 in range(0, tc, rows):
        w0 = min(r0, tc + 2 * HALO - win_len)
        out_row = lax.broadcasted_iota(jnp.int32, (rows, win_len), 0) + (HALO + r0 - w0 - pad)
        win_row = lax.broadcasted_iota(jnp.int32, (rows, win_len), 1)
        sel = jnp.concatenate([jnp.where(win_row == out_row + k, 1.0, 0.0).astype(BF16) for k in taps], axis=0)
        shifted = jnp.dot(sel, ext_ref[pl.ds(w0, win_len), :], preferred_element_type=F32)
        acc = b_ref[...] + w_ref[pad:pad + 1, :] * ext_ref[pl.ds(HALO + r0, rows), :].astype(F32)
        for n, k in enumerate(taps):
            acc = acc + w_ref[k:k + 1, :] * shifted[n * rows:(n + 1) * rows]
        y = _silu(acc).astype(BF16)
        xs_ref[0, pl.ds(r0, rows), :] = y[:, :D_INNER]
        bm_ref[0, pl.ds(r0, rows), :] = y[:, D_INNER:D_INNER + BC_DIM]
        cm_ref[0, pl.ds(r0, rows), :] = y[:, D_INNER + BC_DIM:]


def _conv(xbc, w, bias):
    b, s, _ = xbc.shape
    tc = _token_tile(s, 256)
    nb = tc // HALO
    last = s // HALO - 1

    def tok(width):
        return pl.BlockSpec((1, tc, width), lambda i, j: (i, j, 0))

    return pl.pallas_call(
        _conv_kernel,
        grid=(b, s // tc),
        in_specs=[pl.BlockSpec((1, HALO, CONV_DIM), lambda i, j: (i, jnp.maximum(j * nb - 1, 0), 0)),
                  tok(CONV_DIM),
                  pl.BlockSpec((1, HALO, CONV_DIM), lambda i, j: (i, jnp.minimum((j + 1) * nb, last), 0)),
                  _const_spec((D_CONV, CONV_DIM)), _const_spec((1, CONV_DIM))],
        out_specs=[tok(D_INNER), tok(BC_DIM), tok(BC_DIM)],
        out_shape=[jax.ShapeDtypeStruct((b, s, D_INNER), BF16), jax.ShapeDtypeStruct((b, s, BC_DIM), BF16),
                   jax.ShapeDtypeStruct((b, s, BC_DIM), BF16)],
        scratch_shapes=[pltpu.VMEM((tc + 2 * HALO, CONV_DIM), BF16)],
        compiler_params=_cparams(("parallel", "parallel")),
        name="ssm_conv",
    )(xbc, xbc, xbc, w, bias)


def _split3(v):
    hi = v.astype(BF16)
    r1 = v - hi.astype(F32)
    mid = r1.astype(BF16)
    lo = (r1 - mid.astype(F32)).astype(BF16)
    return hi, mid, lo


def _cumsum_rows(x, lower):
    return sum(jnp.dot(lower, p, preferred_element_type=F32) for p in _split3(x))


def _cumsum_lanes(x, upper):
    return sum(jnp.dot(p, upper, preferred_element_type=F32) for p in _split3(x))


def _expand_heads(v, e):
    hi = v.astype(BF16)
    lo = (v - hi.astype(F32)).astype(BF16)
    return jnp.dot(hi, e, preferred_element_type=F32) + jnp.dot(lo, e, preferred_element_type=F32)


def _ssd_kernel(*refs, reverse, final):
    xs_ref, st_ref = refs[0], refs[-1]

    @pl.when(pl.program_id(1) == 0)
    def _():
        st_ref[...] = jnp.zeros_like(st_ref)

    n_sub = xs_ref.shape[1] // CHUNK
    for sub in (range(n_sub - 1, -1, -1) if reverse else range(n_sub)):
        _ssd_chunk(refs, slice(sub * CHUNK, (sub + 1) * CHUNK), reverse=reverse, final=final)


def _ssd_chunk(refs, rs, *, reverse, final):
    if final:
        (xs_ref, bm_ref, cm_ref, dt_ref, dtt_ref, brow_ref, bcol_ref, arow_ref, acol_ref, e_ref,
         prev_ref, dskip_ref, y_ref, st_ref) = refs
    else:
        (xs_ref, bm_ref, cm_ref, dt_ref, dtt_ref, brow_ref, bcol_ref, arow_ref, acol_ref, e_ref,
         y_ref, st_ref) = refs
    d = 1 if reverse else 0
    hs = slice(d * SSM_HEADS, (d + 1) * SSM_HEADS)
    dt_c = jax.nn.softplus(dt_ref[0, rs, hs] + brow_ref[d:d + 1, :])
    dt_r = jax.nn.softplus(dtt_ref[0, hs, rs] + bcol_ref[d])
    da_c = dt_c * arow_ref[d:d + 1, :]
    da_r = dt_r * acol_ref[d]
    ii = lax.broadcasted_iota(jnp.int32, (CHUNK, CHUNK), 0)
    jj = lax.broadcasted_iota(jnp.int32, (CHUNK, CHUNK), 1)
    cs_c = _cumsum_rows(da_c, jnp.where(jj <= ii, 1.0, 0.0).astype(BF16))
    cs_r = _cumsum_lanes(da_r, jnp.where(ii <= jj, 1.0, 0.0).astype(BF16))
    tot = cs_c[CHUNK - 1:CHUNK, :]
    if reverse:
        a_c = da_c - cs_c
        a_r = da_r - cs_r
        off_scale = jnp.exp(tot + a_c)
        st_w = jnp.exp(-a_c) * dt_c
    else:
        a_c = cs_c
        a_r = cs_r
        off_scale = jnp.exp(cs_c)
        st_w = jnp.exp(tot - cs_c) * dt_c
    e = e_ref[...]
    off_x = _expand_heads(off_scale, e)
    stw_x = _expand_heads(st_w, e)
    dec_x = _expand_heads(jnp.broadcast_to(jnp.exp(tot), (HALO, SSM_HEADS)), e)[0:1, :]

    keep = (jj >= ii) if reverse else (ii >= jj)
    first_head = jj < SSM_HEAD_DIM

    for g in range(SSM_GROUPS):
        gs = slice(g * D_STATE, (g + 1) * D_STATE)
        ws = slice(g * GROUP_WIDTH, (g + 1) * GROUP_WIDTH)
        bg = bm_ref[0, rs, gs]
        cg = cm_ref[0, rs, gs]
        cb = lax.dot_general(cg, bg, NT_DIMS, preferred_element_type=F32)
        st = st_ref[:, ws]
        y_off = jnp.dot(cg, st.astype(BF16), preferred_element_type=F32) * off_x[:, ws]
        xw = (xs_ref[0, rs, ws].astype(F32) * stw_x[:, ws]).astype(BF16)
        st_ref[:, ws] = st * dec_x[:, ws] + lax.dot_general(bg, xw, TN_DIMS, preferred_element_type=F32)
        for pr in range(HEADS_PER_GROUP // 2):
            h0 = g * HEADS_PER_GROUP + 2 * pr
            ps = slice(h0 * SSM_HEAD_DIM, (h0 + 2) * SSM_HEAD_DIM)
            x_p = xs_ref[0, rs, ps]
            ys = []
            for h in (h0, h0 + 1):
                seg = a_c[:, h:h + 1] - a_r[h:h + 1, :]
                w = cb * jnp.exp(jnp.where(keep, seg, -jnp.inf)) * dt_r[h:h + 1, :]
                ys.append(jnp.dot(w.astype(BF16), x_p, preferred_element_type=F32))
            y_p = jnp.where(first_head, ys[0], ys[1]) + y_off[:, pr * 2 * SSM_HEAD_DIM:(pr + 1) * 2 * SSM_HEAD_DIM]
            if final:
                y_p = y_p + prev_ref[0, rs, ps].astype(F32) + x_p.astype(F32) * dskip_ref[:, ps]
            y_ref[0, rs, ps] = y_p.astype(BF16)


def _ssd(xs, bm, cm, dt, dtt, brow, bcol, arow, acol, e, prev=None, dskip=None, *, reverse):
    b, s, _ = xs.shape
    rows = _token_tile(s, SSD_CHUNKS_PER_STEP * CHUNK)
    nc = s // rows
    final = prev is not None
    ndt = 2 * SSM_HEADS

    def cidx(j):
        return nc - 1 - j if reverse else j

    def tok(width):
        return pl.BlockSpec((1, rows, width), lambda i, j: (i, cidx(j), 0))

    in_specs = [tok(D_INNER), tok(BC_DIM), tok(BC_DIM), tok(ndt),
                pl.BlockSpec((1, ndt, rows), lambda i, j: (i, 0, cidx(j))),
                _const_spec((2, SSM_HEADS)), _const_spec((2, SSM_HEADS, 1)),
                _const_spec((2, SSM_HEADS)), _const_spec((2, SSM_HEADS, 1)),
                _const_spec((SSM_HEADS, D_INNER))]
    args = [xs, bm, cm, dt, dtt, brow, bcol, arow, acol, e]
    if final:
        in_specs += [tok(D_INNER), _const_spec((1, D_INNER))]
        args += [prev, dskip]
    return pl.pallas_call(
        functools.partial(_ssd_kernel, reverse=reverse, final=final),
        grid=(b, nc),
        in_specs=in_specs,
        out_specs=tok(D_INNER),
        out_shape=jax.ShapeDtypeStruct((b, s, D_INNER), BF16),
        scratch_shapes=[pltpu.VMEM((D_STATE, D_INNER), F32)],
        compiler_params=_cparams(("parallel", "arbitrary")),
        name="ssd_bwd" if reverse else "ssd_fwd",
    )(*args)


def _ssm_out_kernel(x_ref, y_ref, z_ref, ng_ref, w_ref, gpost_ref, o_ref):
    y = y_ref[...].astype(F32) * _silu(z_ref[...].astype(F32))
    parts = []
    for g in range(SSM_GROUPS):
        yg = y[:, g * GROUP_WIDTH:(g + 1) * GROUP_WIDTH]
        parts.append(yg * lax.rsqrt(jnp.mean(yg * yg, axis=-1, keepdims=True) + EPS))
    yn = (jnp.concatenate(parts, axis=-1) * ng_ref[...]).astype(BF16)
    m = jnp.dot(yn, w_ref[...], preferred_element_type=F32)
    o_ref[...] = x_ref[...] + _rms(m, gpost_ref[...])


def _ssm_out(x2, y2, z2, ng, w, g_post):
    n = x2.shape[0]
    tm = _token_tile(n, 512)

    def row(width):
        return pl.BlockSpec((tm, width), lambda i: (i, 0))

    return pl.pallas_call(
        _ssm_out_kernel,
        grid=(n // tm,),
        in_specs=[row(D_MODEL), row(D_INNER), row(D_INNER), _const_spec((1, D_INNER)),
                  _const_spec((D_INNER, D_MODEL)), _const_spec((1, D_MODEL))],
        out_specs=row(D_MODEL),
        out_shape=jax.ShapeDtypeStruct((n, D_MODEL), F32),
        compiler_params=_cparams(("parallel",)),
        name="ssm_out",
    )(x2, y2, z2, ng, w, g_post)


def _qkv_kernel(x_ref, g_ref, w_ref, o_ref):
    u = _rms(x_ref[...], g_ref[...]).astype(BF16)
    o_ref[...] = jnp.dot(u, w_ref[...], preferred_element_type=F32).astype(BF16)


def _qkv(x2, g, w):
    n = x2.shape[0]
    tm = _token_tile(n, 512)
    return pl.pallas_call(
        _qkv_kernel,
        grid=(n // tm,),
        in_specs=[pl.BlockSpec((tm, D_MODEL), lambda i: (i, 0)), _const_spec((1, D_MODEL)),
                  _const_spec((D_MODEL, 3 * ATTN_DIM))],
        out_specs=pl.BlockSpec((tm, 3 * ATTN_DIM), lambda i: (i, 0)),
        out_shape=jax.ShapeDtypeStruct((n, 3 * ATTN_DIM), BF16),
        compiler_params=_cparams(("parallel",)),
        name="attn_qkv",
    )(x2, g, w)


def _attn_kernel(lam_ref, tbl_ref, bucket_ref, q_ref, k_ref, v_ref, g_ref, o_ref, slab_ref, s_ref, p_ref, vv_ref,
                 q2_ref, *, out_scale):
    tq = slab_ref.shape[0]
    s = k_ref.shape[1]
    nk = s // tq
    h = pl.program_id(0)

    @pl.when(pl.program_id(1) == 0)
    def _():
        bk = bucket_ref[...]
        acc = jnp.zeros(bk.shape, F32)
        for kk in range(N_BUCKETS):
            acc = jnp.where(bk == kk, tbl_ref[h, kk], acc)
        slab_ref[...] = acc

    lam = lam_ref[0]
    g = g_ref[...]
    vv_ref[...] = jnp.concatenate([v_ref[0], jnp.ones((s, ATTN_WIDTH), BF16)], axis=1)
    kchunk = 2 * tq if s % (2 * tq) == 0 else tq
    rblk = min(ATTN_ROW_BLOCK, tq)

    def q_tile(qi, carry):
        q0 = pl.multiple_of(qi * tq, tq)
        q = q_ref[0, pl.ds(q0, tq), :]
        lane = lax.broadcasted_iota(jnp.int32, q.shape, 1)
        zero = jnp.zeros_like(q)
        q2_ref[:tq, :] = jnp.where(lane < ATTN_HEAD_DIM, q, zero)
        q2_ref[tq:, :] = jnp.where(lane >= ATTN_HEAD_DIM, q, zero)
        mxs = []
        for r0 in range(0, 2 * tq, rblk):
            a0 = r0 % tq
            mrun = jnp.full((rblk, 128), -jnp.inf, F32)
            for k0 in range(0, s, kchunk):
                tiles = []
                for kc in range(k0 // tq, (k0 + kchunk) // tq):
                    d = jnp.clip(kc - qi, -SLAB_REACH, SLAB_REACH) + SLAB_REACH
                    tiles.append(slab_ref[a0:a0 + rblk, pl.ds(pl.multiple_of(d * tq, tq), tq)])
                bias = jnp.concatenate(tiles, axis=1) if len(tiles) > 1 else tiles[0]
                sc = lax.dot_general(q2_ref[r0:r0 + rblk, :], k_ref[0, k0:k0 + kchunk, :], NT_DIMS,
                                     preferred_element_type=F32) + bias
                s_ref[r0:r0 + rblk, k0:k0 + kchunk] = sc
                for j in range(kchunk // 128):
                    mrun = jnp.maximum(mrun, sc[:, j * 128:(j + 1) * 128])
            mxs.append(jnp.max(mrun, axis=-1, keepdims=True))
        mx = jnp.concatenate(mxs, axis=0)
        p_ref[...] = jnp.exp2(s_ref[...] - mx).astype(BF16)
        vv = vv_ref[...]
        acc1 = jnp.dot(p_ref[:tq, :], vv, preferred_element_type=F32)
        acc2 = jnp.dot(p_ref[tq:, :], vv, preferred_element_type=F32)
        o1 = acc1[:, :ATTN_WIDTH] / acc1[:, ATTN_WIDTH:ATTN_WIDTH + 1]
        o2 = acc2[:, :ATTN_WIDTH] / acc2[:, ATTN_WIDTH:ATTN_WIDTH + 1]
        o_ref[0, pl.ds(q0, tq), :] = (_rms(o1 - lam * o2, g) * out_scale).astype(BF16)
        return carry

    lax.fori_loop(0, s // tq, q_tile, 0, unroll=2)


def _attention(lam, tbl, bucket, qkv, subln, *, out_scale):
    b, s, _ = qkv.shape
    tq = bucket.shape[0]

    def head_block(col0):
        return pl.BlockSpec((1, s, ATTN_WIDTH), lambda h, i: (i, 0, col0 + h))

    return pl.pallas_call(
        functools.partial(_attn_kernel, out_scale=out_scale),
        grid=(ATTN_HEADS, b),
        in_specs=[pl.BlockSpec(memory_space=pltpu.SMEM), pl.BlockSpec(memory_space=pltpu.SMEM),
                  _const_spec(bucket.shape), head_block(0), head_block(ATTN_HEADS), head_block(2 * ATTN_HEADS),
                  _const_spec((1, ATTN_WIDTH))],
        out_specs=head_block(0),
        out_shape=jax.ShapeDtypeStruct((b, s, ATTN_DIM), BF16),
        scratch_shapes=[pltpu.VMEM(bucket.shape, F32), pltpu.VMEM((2 * tq, s), F32), pltpu.VMEM((2 * tq, s), BF16),
                        pltpu.VMEM((s, 2 * ATTN_WIDTH), BF16), pltpu.VMEM((2 * tq, ATTN_WIDTH), BF16)],
        compiler_params=_cparams(("arbitrary", "arbitrary")),
        name="diff_attn",
    )(lam, tbl, bucket, qkv, qkv, qkv, subln)


def _proj_out_kernel(x_ref, a_ref, w_ref, gpost_ref, o_ref):
    m = jnp.dot(a_ref[...], w_ref[...], preferred_element_type=F32)
    o_ref[...] = x_ref[...] + _rms(m, gpost_ref[...])


def _proj_out(x2, a2, w, g_post):
    n = x2.shape[0]
    k = a2.shape[1]
    tm = _token_tile(n, 512)
    return pl.pallas_call(
        _proj_out_kernel,
        grid=(n // tm,),
        in_specs=[pl.BlockSpec((tm, D_MODEL), lambda i: (i, 0)), pl.BlockSpec((tm, k), lambda i: (i, 0)),
                  _const_spec((k, D_MODEL)), _const_spec((1, D_MODEL))],
        out_specs=pl.BlockSpec((tm, D_MODEL), lambda i: (i, 0)),
        out_shape=jax.ShapeDtypeStruct((n, D_MODEL), F32),
        compiler_params=_cparams(("parallel",)),
        name="proj_out",
    )(x2, a2, w, g_post)


def _relative_bucket(rel):
    half = N_BUCKETS // 2
    max_exact = half // 2
    ret = jnp.where(rel > 0, half, 0)
    n = jnp.abs(rel)
    nf = jnp.maximum(n, 1).astype(F32)
    large = max_exact + (jnp.log(nf / max_exact) / math.log(MAX_DISTANCE / max_exact) * (half - max_exact)).astype(jnp.int32)
    large = jnp.minimum(large, half - 1)
    return ret + jnp.where(n < max_exact, n, large)


def _bucket_slab(tq):
    a = jnp.arange(tq)[:, None]
    c = jnp.arange((2 * SLAB_REACH + 1) * tq)[None, :]
    return _relative_bucket(c - a - SLAB_REACH * tq).astype(jnp.int32)


def _row(v):
    return v.reshape(1, -1).astype(F32)


def _mamba_layer(x, pre_g, post_g, w_in, conv_w, conv_b, dt_bias, a_log, d_skip, norm_g, w_out):
    b, s, _ = x.shape
    wz = w_in[:, :D_INNER].astype(BF16)
    wx = w_in[:, D_INNER:D_INNER + CONV_DIM].astype(BF16)
    wdt = w_in[:, D_INNER + CONV_DIM:].astype(BF16)
    z, xbc, dt, dtt = _ssm_in(x, _row(pre_g), wz, wx, wdt, wdt.T)
    xs, bm, cm = _conv(xbc, conv_w.astype(F32), _row(conv_b))
    a = -jnp.exp(a_log.astype(F32))
    brow = dt_bias.astype(F32)
    e = jnp.repeat(jnp.eye(SSM_HEADS, dtype=BF16), SSM_HEAD_DIM, axis=1)
    dskip = jnp.repeat(d_skip.astype(F32), SSM_HEAD_DIM).reshape(1, D_INNER)
    common = (xs, bm, cm, dt, dtt, brow, brow[:, :, None], a, a[:, :, None], e)
    y_fwd = _ssd(*common, reverse=False)
    y = _ssd(*common, y_fwd, dskip, reverse=True)
    out = _ssm_out(x.reshape(b * s, D_MODEL), y.reshape(b * s, D_INNER), z.reshape(b * s, D_INNER),
                   _row(norm_g), w_out.astype(BF16), _row(post_g))
    return out.reshape(b, s, D_MODEL)


def _attn_layer(x, pre_g, post_g, w_qkv, lam, subln_g, w_out, rel_bias, lambda_init):
    b, s, _ = x.shape
    scale = ATTN_HEAD_DIM ** -0.5 * LOG2E
    col_scale = jnp.concatenate([jnp.full((ATTN_DIM,), scale, F32), jnp.ones((2 * ATTN_DIM,), F32)])
    w = (w_qkv.astype(F32) * col_scale).astype(BF16)
    qkv = _qkv(x.reshape(b * s, D_MODEL), _row(pre_g), w).reshape(b, s, 3 * ATTN_DIM)
    lf = lam.astype(F32)
    lam_full = jnp.exp(jnp.sum(lf[0] * lf[1])) - jnp.exp(jnp.sum(lf[2] * lf[3])) + lambda_init
    tq = _token_tile(s, ATTN_Q_TILE)
    assert tq >= MAX_DISTANCE, "bias tiles two q tiles away must lie in the saturated buckets"
    tbl = rel_bias.astype(F32).T * LOG2E
    o = _attention(lam_full.reshape(1), tbl, _bucket_slab(tq), qkv, _row(subln_g), out_scale=1.0 - lambda_init)
    out = _proj_out(x.reshape(b * s, D_MODEL), o.reshape(b * s, ATTN_DIM), w_out.astype(BF16), _row(post_g))
    return out.reshape(b, s, D_MODEL)


def _trunk(x, norm_pre, norm_post, ffn_w_gate, ffn_w_up, ffn_w_down, ssm_w_in, ssm_conv_w, ssm_conv_b,
           ssm_dt_bias, ssm_a_log, ssm_d, ssm_norm, ssm_w_out, attn_w_qkv, attn_lambda, attn_subln,
           attn_w_out, rel_bias):
    b, s, _ = x.shape
    depth = norm_pre.shape[0]

    def ffn(x, i, k, slot):
        y = _ffn(x.reshape(b * s, D_MODEL), _row(norm_pre[i, slot]), ffn_w_gate[i, k].astype(BF16),
                 ffn_w_up[i, k].astype(BF16), ffn_w_down[i, k].astype(BF16), _row(norm_post[i, slot]))
        return y.reshape(b, s, D_MODEL)

    for i in range(depth):
        x = ffn(x, i, 0, 0)
        j = i // 2
        if i % 2 == 0:
            x = _mamba_layer(x, norm_pre[i, 1], norm_post[i, 1], ssm_w_in[j], ssm_conv_w[j], ssm_conv_b[j],
                             ssm_dt_bias[j], ssm_a_log[j], ssm_d[j], ssm_norm[j], ssm_w_out[j])
        else:
            lambda_init = 0.8 - 0.6 * math.exp(-0.3 * i)
            x = _attn_layer(x, norm_pre[i, 1], norm_post[i, 1], attn_w_qkv[j], attn_lambda[j], attn_subln[j],
                            attn_w_out[j], rel_bias, lambda_init)
        x = ffn(x, i, 1, 2)
    return x


def kernel(x_prompt, x_sample, norm_pre, norm_post, ffn_w_gate, ffn_w_up, ffn_w_down, ssm_w_in, ssm_conv_w, ssm_conv_b, ssm_dt_bias, ssm_a_log, ssm_d, ssm_norm, ssm_w_out, attn_w_qkv, attn_lambda, attn_subln, attn_w_out, rel_bias):
    params = (norm_pre, norm_post, ffn_w_gate, ffn_w_up, ffn_w_down, ssm_w_in, ssm_conv_w, ssm_conv_b,
              ssm_dt_bias, ssm_a_log, ssm_d, ssm_norm, ssm_w_out, attn_w_qkv, attn_lambda, attn_subln,
              attn_w_out, rel_bias)
    return (_trunk(x_prompt, *params), _trunk(x_sample, *params))
```

```python
import functools
import math

import jax
import jax.numpy as jnp
from jax import lax
from jax.experimental import pallas as pl
from jax.experimental.pallas import tpu as pltpu

F32 = jnp.float32
BF16 = jnp.bfloat16

EPS = 1e-6
D_MODEL = 1024
D_FF = 2816
FF_CHUNK = 256
FFN_TOKEN_TILE = 512

D_INNER = 2048
SSM_HEAD_DIM = 64
SSM_HEADS = 32
SSM_GROUPS = 4
HEADS_PER_GROUP = SSM_HEADS // SSM_GROUPS
D_STATE = 128
D_CONV = 5
CHUNK = 128
SSD_CHUNKS_PER_STEP = 4
BC_DIM = SSM_GROUPS * D_STATE
CONV_DIM = D_INNER + 2 * BC_DIM
GROUP_WIDTH = D_INNER // SSM_GROUPS

ATTN_HEADS = 8
ATTN_HEAD_DIM = 64
ATTN_WIDTH = 2 * ATTN_HEAD_DIM
ATTN_DIM = ATTN_HEADS * ATTN_WIDTH
N_BUCKETS = 32
MAX_DISTANCE = 128
ATTN_Q_TILE = 512
ATTN_Q_UNROLL = 4
ATTN_ROW_BLOCK = 128
SLAB_REACH = 2
LOG2E = math.log2(math.e)

HALO = 16
CONV_ROWS = 128
CONV_WINDOW = 256
VMEM_LIMIT = 56 * 1024 * 1024

NT_DIMS = (((1,), (1,)), ((), ()))
TN_DIMS = (((0,), (0,)), ((), ()))


def _cparams(sem):
    return pltpu.CompilerParams(dimension_semantics=sem, vmem_limit_bytes=VMEM_LIMIT)


def _rms(x, g):
    return x * lax.rsqrt(jnp.mean(x * x, axis=-1, keepdims=True) + EPS) * g


def _silu(x):
    return x * (1.0 / (1.0 + jnp.exp(-x)))


def _const_spec(shape):
    nd = len(shape)
    return pl.BlockSpec(shape, lambda *_: (0,) * nd, pipeline_mode=pl.Buffered(1))


def _token_tile(n, pref):
    t = min(pref, n)
    while n % t:
        t //= 2
    return t


def _ffn_kernel(x_ref, gpre_ref, wg_ref, wu_ref, wd_ref, gpost_ref, o_ref, xn_ref, acc_ref):
    xn_ref[...] = _rms(x_ref[...], gpre_ref[...]).astype(BF16)
    for c in range(D_FF // FF_CHUNK):
        sl = slice(c * FF_CHUNK, (c + 1) * FF_CHUNK)
        xn = xn_ref[...]
        g = jnp.dot(xn, wg_ref[:, sl], preferred_element_type=F32)
        u = jnp.dot(xn, wu_ref[:, sl], preferred_element_type=F32)
        h = (_silu(g) * u).astype(BF16)
        d = jnp.dot(h, wd_ref[sl, :], preferred_element_type=F32)
        if c == 0:
            acc_ref[...] = d
        else:
            acc_ref[...] += d
    o_ref[...] = x_ref[...] + 0.5 * _rms(acc_ref[...], gpost_ref[...])


def _ffn(x2, g_pre, wg, wu, wd, g_post):
    n = x2.shape[0]
    tm = _token_tile(n, FFN_TOKEN_TILE)
    row = pl.BlockSpec((tm, D_MODEL), lambda i: (i, 0))
    return pl.pallas_call(
        _ffn_kernel,
        grid=(n // tm,),
        in_specs=[row, _const_spec((1, D_MODEL)), _const_spec((D_MODEL, D_FF)), _const_spec((D_MODEL, D_FF)),
                  _const_spec((D_FF, D_MODEL)), _const_spec((1, D_MODEL))],
        out_specs=row,
        out_shape=jax.ShapeDtypeStruct((n, D_MODEL), F32),
        scratch_shapes=[pltpu.VMEM((tm, D_MODEL), BF16), pltpu.VMEM((tm, D_MODEL), F32)],
        compiler_params=_cparams(("parallel",)),
        name="ffn",
    )(x2, g_pre, wg, wu, wd, g_post)


def _ssm_in_kernel(x_ref, g_ref, wz_ref, wx_ref, wdt_ref, wdtt_ref, z_ref, xbc_ref, dt_ref, dtt_ref):
    u = _rms(x_ref[0], g_ref[...]).astype(BF16)
    z_ref[0] = jnp.dot(u, wz_ref[...], preferred_element_type=F32).astype(BF16)
    xbc_ref[0] = jnp.dot(u, wx_ref[...], preferred_element_type=F32).astype(BF16)
    dt_ref[0] = jnp.dot(u, wdt_ref[...], preferred_element_type=F32)
    dtt_ref[0] = lax.dot_general(wdtt_ref[...], u, NT_DIMS, preferred_element_type=F32)


def _ssm_in(x, g, wz, wx, wdt, wdtt):
    b, s, _ = x.shape
    tm = _token_tile(s, 512)
    ndt = 2 * SSM_HEADS

    def tok(width):
        return pl.BlockSpec((1, tm, width), lambda i, j: (i, j, 0))

    return pl.pallas_call(
        _ssm_in_kernel,
        grid=(b, s // tm),
        in_specs=[tok(D_MODEL), _const_spec((1, D_MODEL)), _const_spec((D_MODEL, D_INNER)),
                  _const_spec((D_MODEL, CONV_DIM)), _const_spec((D_MODEL, ndt)), _const_spec((ndt, D_MODEL))],
        out_specs=[tok(D_INNER), tok(CONV_DIM), tok(ndt), pl.BlockSpec((1, ndt, tm), lambda i, j: (i, 0, j))],
        out_shape=[jax.ShapeDtypeStruct((b, s, D_INNER), BF16), jax.ShapeDtypeStruct((b, s, CONV_DIM), BF16),
                   jax.ShapeDtypeStruct((b, s, ndt), F32), jax.ShapeDtypeStruct((b, ndt, s), F32)],
        compiler_params=_cparams(("parallel", "parallel")),
        name="ssm_in",
    )(x, g, wz, wx, wdt, wdtt)


def _conv_kernel(prev_ref, main_ref, next_ref, w_ref, b_ref, xs_ref, bm_ref, cm_ref, ext_ref):
    j = pl.program_id(1)
    tc = main_ref.shape[1]
    zero = jnp.zeros((HALO, CONV_DIM), BF16)
    ext_ref[pl.ds(0, HALO), :] = jnp.where(j > 0, prev_ref[0], zero)
    ext_ref[pl.ds(HALO, tc), :] = main_ref[0]
    ext_ref[pl.ds(HALO + tc, HALO), :] = jnp.where(j < pl.num_programs(1) - 1, next_ref[0], zero)
    pad = D_CONV // 2
    rows = min(CONV_ROWS, tc)
    win_len = min(CONV_WINDOW, tc + 2 * HALO)
    taps = [k for k in range(D_CONV) if k != pad]
    for r0 in range(0, tc, rows):
        w0 = min(r0, tc + 2 * HALO - win_len)
        out_row = lax.broadcasted_iota(jnp.int32, (rows, win_len), 0) + (HALO + r0 - w0 - pad)
        win_row = lax.broadcasted_iota(jnp.int32, (rows, win_len), 1)
        sel = jnp.concatenate([jnp.where(win_row == out_row + k, 1.0, 0.0).astype(BF16) for k in taps], axis=0)
        shifted = jnp.dot(sel, ext_ref[pl.ds(w0, win_len), :], preferred_element_type=F32)
        acc = b_ref[...] + w_ref[pad:pad + 1, :] * ext_ref[pl.ds(HALO + r0, rows), :].astype(F32)
        for n, k in enumerate(taps):
            acc = acc + w_ref[k:k + 1, :] * shifted[n * rows:(n + 1) * rows]
        y = _silu(acc).astype(BF16)
        xs_ref[0, pl.ds(r0, rows), :] = y[:, :D_INNER]
        bm_ref[0, pl.ds(r0, rows), :] = y[:, D_INNER:D_INNER + BC_DIM]
        cm_ref[0, pl.ds(r0, rows), :] = y[:, D_INNER + BC_DIM:]


def _conv(xbc, w, bias):
    b, s, _ = xbc.shape
    tc = _token_tile(s, 256)
    nb = tc // HALO
    last = s // HALO - 1

    def tok(width):
        return pl.BlockSpec((1, tc, width), lambda i, j: (i, j, 0))

    return pl.pallas_call(
        _conv_kernel,
        grid=(b, s // tc),
        in_specs=[pl.BlockSpec((1, HALO, CONV_DIM), lambda i, j: (i, jnp.maximum(j * nb - 1, 0), 0)),
                  tok(CONV_DIM),
                  pl.BlockSpec((1, HALO, CONV_DIM), lambda i, j: (i, jnp.minimum((j + 1) * nb, last), 0)),
                  _const_spec((D_CONV, CONV_DIM)), _const_spec((1, CONV_DIM))],
        out_specs=[tok(D_INNER), tok(BC_DIM), tok(BC_DIM)],
        out_shape=[jax.ShapeDtypeStruct((b, s, D_INNER), BF16), jax.ShapeDtypeStruct((b, s, BC_DIM), BF16),
                   jax.ShapeDtypeStruct((b, s, BC_DIM), BF16)],
        scratch_shapes=[pltpu.VMEM((tc + 2 * HALO, CONV_DIM), BF16)],
        compiler_params=_cparams(("parallel", "parallel")),
        name="ssm_conv",
    )(xbc, xbc, xbc, w, bias)


def _split3(v):
    hi = v.astype(BF16)
    r1 = v - hi.astype(F32)
    mid = r1.astype(BF16)
    lo = (r1 - mid.astype(F32)).astype(BF16)
    return hi, mid, lo


def _cumsum_rows(x, lower):
    return sum(jnp.dot(lower, p, preferred_element_type=F32) for p in _split3(x))


def _cumsum_lanes(x, upper):
    return sum(jnp.dot(p, upper, preferred_element_type=F32) for p in _split3(x))


def _expand_heads(v, e):
    hi = v.astype(BF16)
    lo = (v - hi.astype(F32)).astype(BF16)
    return jnp.dot(hi, e, preferred_element_type=F32) + jnp.dot(lo, e, preferred_element_type=F32)


def _ssd_kernel(*refs, reverse, final):
    xs_ref, st_ref = refs[0], refs[-1]

    @pl.when(pl.program_id(1) == 0)
    def _():
        st_ref[...] = jnp.zeros_like(st_ref)

    n_sub = xs_ref.shape[1] // CHUNK
    for sub in (range(n_sub - 1, -1, -1) if reverse else range(n_sub)):
        _ssd_chunk(refs, slice(sub * CHUNK, (sub + 1) * CHUNK), reverse=reverse, final=final)


def _ssd_chunk(refs, rs, *, reverse, final):
    if final:
        (xs_ref, bm_ref, cm_ref, dt_ref, dtt_ref, brow_ref, bcol_ref, arow_ref, acol_ref, e_ref,
         prev_ref, dskip_ref, y_ref, st_ref) = refs
    else:
        (xs_ref, bm_ref, cm_ref, dt_ref, dtt_ref, brow_ref, bcol_ref, arow_ref, acol_ref, e_ref,
         y_ref, st_ref) = refs
    d = 1 if reverse else 0
    hs = slice(d * SSM_HEADS, (d + 1) * SSM_HEADS)
    dt_c = jax.nn.softplus(dt_ref[0, rs, hs] + brow_ref[d:d + 1, :])
    dt_r = jax.nn.softplus(dtt_ref[0, hs, rs] + bcol_ref[d])
    da_c = dt_c * arow_ref[d:d + 1, :]
    da_r = dt_r * acol_ref[d]
    ii = lax.broadcasted_iota(jnp.int32, (CHUNK, CHUNK), 0)
    jj = lax.broadcasted_iota(jnp.int32, (CHUNK, CHUNK), 1)
    cs_c = _cumsum_rows(da_c, jnp.where(jj <= ii, 1.0, 0.0).astype(BF16))
    cs_r = _cumsum_lanes(da_r, jnp.where(ii <= jj, 1.0, 0.0).astype(BF16))
    tot = cs_c[CHUNK - 1:CHUNK, :]
    if reverse:
        a_c = da_c - cs_c
        a_r = da_r - cs_r
        off_scale = jnp.exp(tot + a_c)
        st_w = jnp.exp(-a_c) * dt_c
    else:
        a_c = cs_c
        a_r = cs_r
        off_scale = jnp.exp(cs_c)
        st_w = jnp.exp(tot - cs_c) * dt_c
    e = e_ref[...]
    off_x = _expand_heads(off_scale, e)
    stw_x = _expand_heads(st_w, e)
    dec_x = _expand_heads(jnp.broadcast_to(jnp.exp(tot), (HALO, SSM_HEADS)), e)[0:1, :]

    keep = (jj >= ii) if reverse else (ii >= jj)
    first_head = jj < SSM_HEAD_DIM

    for g in range(SSM_GROUPS):
        gs = slice(g * D_STATE, (g + 1) * D_STATE)
        ws = slice(g * GROUP_WIDTH, (g + 1) * GROUP_WIDTH)
        bg = bm_ref[0, rs, gs]
        cg = cm_ref[0, rs, gs]
        cb = lax.dot_general(cg, bg, NT_DIMS, preferred_element_type=F32)
        st = st_ref[:, ws]
        y_off = jnp.dot(cg, st.astype(BF16), preferred_element_type=F32) * off_x[:, ws]
        xw = (xs_ref[0, rs, ws].astype(F32) * stw_x[:, ws]).astype(BF16)
        st_ref[:, ws] = st * dec_x[:, ws] + lax.dot_general(bg, xw, TN_DIMS, preferred_element_type=F32)
        for pr in range(HEADS_PER_GROUP // 2):
            h0 = g * HEADS_PER_GROUP + 2 * pr
            ps = slice(h0 * SSM_HEAD_DIM, (h0 + 2) * SSM_HEAD_DIM)
            x_p = xs_ref[0, rs, ps]
            ys = []
            for h in (h0, h0 + 1):
                seg = a_c[:, h:h + 1] - a_r[h:h + 1, :]
                w = cb * jnp.exp(jnp.where(keep, seg, -jnp.inf)) * dt_r[h:h + 1, :]
                ys.append(jnp.dot(w.astype(BF16), x_p, preferred_element_type=F32))
            y_p = jnp.where(first_head, ys[0], ys[1]) + y_off[:, pr * 2 * SSM_HEAD_DIM:(pr + 1) * 2 * SSM_HEAD_DIM]
            if final:
                y_p = y_p + prev_ref[0, rs, ps].astype(F32) + x_p.astype(F32) * dskip_ref[:, ps]
            y_ref[0, rs, ps] = y_p.astype(BF16)


def _ssd(xs, bm, cm, dt, dtt, brow, bcol, arow, acol, e, prev=None, dskip=None, *, reverse):
    b, s, _ = xs.shape
    rows = _token_tile(s, SSD_CHUNKS_PER_STEP * CHUNK)
    nc = s // rows
    final = prev is not None
    ndt = 2 * SSM_HEADS

    def cidx(j):
        return nc - 1 - j if reverse else j

    def tok(width):
        return pl.BlockSpec((1, rows, width), lambda i, j: (i, cidx(j), 0))

    in_specs = [tok(D_INNER), tok(BC_DIM), tok(BC_DIM), tok(ndt),
                pl.BlockSpec((1, ndt, rows), lambda i, j: (i, 0, cidx(j))),
                _const_spec((2, SSM_HEADS)), _const_spec((2, SSM_HEADS, 1)),
                _const_spec((2, SSM_HEADS)), _const_spec((2, SSM_HEADS, 1)),
                _const_spec((SSM_HEADS, D_INNER))]
    args = [xs, bm, cm, dt, dtt, brow, bcol, arow, acol, e]
    if final:
        in_specs += [tok(D_INNER), _const_spec((1, D_INNER))]
        args += [prev, dskip]
    return pl.pallas_call(
        functools.partial(_ssd_kernel, reverse=reverse, final=final),
        grid=(b, nc),
        in_specs=in_specs,
        out_specs=tok(D_INNER),
        out_shape=jax.ShapeDtypeStruct((b, s, D_INNER), BF16),
        scratch_shapes=[pltpu.VMEM((D_STATE, D_INNER), F32)],
        compiler_params=_cparams(("parallel", "arbitrary")),
        name="ssd_bwd" if reverse else "ssd_fwd",
    )(*args)


def _ssm_out_kernel(x_ref, y_ref, z_ref, ng_ref, w_ref, gpost_ref, o_ref):
    y = y_ref[...].astype(F32) * _silu(z_ref[...].astype(F32))
    parts = []
    for g in range(SSM_GROUPS):
        yg = y[:, g * GROUP_WIDTH:(g + 1) * GROUP_WIDTH]
        parts.append(yg * lax.rsqrt(jnp.mean(yg * yg, axis=-1, keepdims=True) + EPS))
    yn = (jnp.concatenate(parts, axis=-1) * ng_ref[...]).astype(BF16)
    m = jnp.dot(yn, w_ref[...], preferred_element_type=F32)
    o_ref[...] = x_ref[...] + _rms(m, gpost_ref[...])


def _ssm_out(x2, y2, z2, ng, w, g_post):
    n = x2.shape[0]
    tm = _token_tile(n, 512)

    def row(width):
        return pl.BlockSpec((tm, width), lambda i: (i, 0))

    return pl.pallas_call(
        _ssm_out_kernel,
        grid=(n // tm,),
        in_specs=[row(D_MODEL), row(D_INNER), row(D_INNER), _const_spec((1, D_INNER)),
                  _const_spec((D_INNER, D_MODEL)), _const_spec((1, D_MODEL))],
        out_specs=row(D_MODEL),
        out_shape=jax.ShapeDtypeStruct((n, D_MODEL), F32),
        compiler_params=_cparams(("parallel",)),
        name="ssm_out",
    )(x2, y2, z2, ng, w, g_post)


def _qkv_kernel(x_ref, g_ref, w_ref, o_ref):
    u = _rms(x_ref[...], g_ref[...]).astype(BF16)
    o_ref[...] = jnp.dot(u, w_ref[...], preferred_element_type=F32).astype(BF16)


def _qkv(x2, g, w):
    n = x2.shape[0]
    tm = _token_tile(n, 512)
    return pl.pallas_call(
        _qkv_kernel,
        grid=(n // tm,),
        in_specs=[pl.BlockSpec((tm, D_MODEL), lambda i: (i, 0)), _const_spec((1, D_MODEL)),
                  _const_spec((D_MODEL, 3 * ATTN_DIM))],
        out_specs=pl.BlockSpec((tm, 3 * ATTN_DIM), lambda i: (i, 0)),
        out_shape=jax.ShapeDtypeStruct((n, 3 * ATTN_DIM), BF16),
        compiler_params=_cparams(("parallel",)),
        name="attn_qkv",
    )(x2, g, w)


def _attn_kernel(lam_ref, tbl_ref, bucket_ref, q_ref, k_ref, v_ref, g_ref, o_ref, slab_ref, s_ref, p_ref, vv_ref,
                 q2_ref, *, out_scale):
    tq = slab_ref.shape[0]
    s = k_ref.shape[1]
    nk = s // tq
    h = pl.program_id(0)

    @pl.when(pl.program_id(1) == 0)
    def _():
        bk = bucket_ref[...]
        acc = jnp.zeros(bk.shape, F32)
        for kk in range(N_BUCKETS):
            acc = jnp.where(bk == kk, tbl_ref[h, kk], acc)
        slab_ref[...] = acc

    lam = lam_ref[0]
    g = g_ref[...]
    vv_ref[...] = jnp.concatenate([v_ref[0], jnp.ones((s, ATTN_WIDTH), BF16)], axis=1)
    kchunk = 2 * tq if s % (2 * tq) == 0 else tq
    rblk = min(ATTN_ROW_BLOCK, tq)

    def q_tile(qi, carry):
        q0 = pl.multiple_of(qi * tq, tq)
        q = q_ref[0, pl.ds(q0, tq), :]
        lane = lax.broadcasted_iota(jnp.int32, q.shape, 1)
        zero = jnp.zeros_like(q)
        q2_ref[:tq, :] = jnp.where(lane < ATTN_HEAD_DIM, q, zero)
        q2_ref[tq:, :] = jnp.where(lane >= ATTN_HEAD_DIM, q, zero)
        mxs = []
        for r0 in range(0, 2 * tq, rblk):
            a0 = r0 % tq
            mrun = jnp.full((rblk, 128), -jnp.inf, F32)
            for k0 in range(0, s, kchunk):
                tiles = []
                for kc in range(k0 // tq, (k0 + kchunk) // tq):
                    d = jnp.clip(kc - qi, -SLAB_REACH, SLAB_REACH) + SLAB_REACH
                    tiles.append(slab_ref[a0:a0 + rblk, pl.ds(pl.multiple_of(d * tq, tq), tq)])
                bias = jnp.concatenate(tiles, axis=1) if len(tiles) > 1 else tiles[0]
                sc = lax.dot_general(q2_ref[r0:r0 + rblk, :], k_ref[0, k0:k0 + kchunk, :], NT_DIMS,
                                     preferred_element_type=F32) + bias
                s_ref[r0:r0 + rblk, k0:k0 + kchunk] = sc
                for j in range(kchunk // 128):
                    mrun = jnp.maximum(mrun, sc[:, j * 128:(j + 1) * 128])
            mxs.append(jnp.max(mrun, axis=-1, keepdims=True))
        mx = jnp.concatenate(mxs, axis=0)
        p_ref[...] = jnp.exp2(s_ref[...] - mx).astype(BF16)
        vv = vv_ref[...]
        acc1 = jnp.dot(p_ref[:tq, :], vv, preferred_element_type=F32)
        acc2 = jnp.dot(p_ref[tq:, :], vv, preferred_element_type=F32)
        o1 = acc1[:, :ATTN_WIDTH] / acc1[:, ATTN_WIDTH:ATTN_WIDTH + 1]
        o2 = acc2[:, :ATTN_WIDTH] / acc2[:, ATTN_WIDTH:ATTN_WIDTH + 1]
        o_ref[0, pl.ds(q0, tq), :] = (_rms(o1 - lam * o2, g) * out_scale).astype(BF16)
        return carry

    nq = s // tq
    lax.fori_loop(0, nq, q_tile, 0, unroll=next(u for u in (ATTN_Q_UNROLL, 2, 1) if nq % u == 0))


def _attention(lam, tbl, bucket, qkv, subln, *, out_scale):
    b, s, _ = qkv.shape
    tq = bucket.shape[0]

    def head_block(col0):
        return pl.BlockSpec((1, s, ATTN_WIDTH), lambda h, i: (i, 0, col0 + h))

    return pl.pallas_call(
        functools.partial(_attn_kernel, out_scale=out_scale),
        grid=(ATTN_HEADS, b),
        in_specs=[pl.BlockSpec(memory_space=pltpu.SMEM), pl.BlockSpec(memory_space=pltpu.SMEM),
                  _const_spec(bucket.shape), head_block(0), head_block(ATTN_HEADS), head_block(2 * ATTN_HEADS),
                  _const_spec((1, ATTN_WIDTH))],
        out_specs=head_block(0),
        out_shape=jax.ShapeDtypeStruct((b, s, ATTN_DIM), BF16),
        scratch_shapes=[pltpu.VMEM(bucket.shape, F32), pltpu.VMEM((2 * tq, s), F32), pltpu.VMEM((2 * tq, s), BF16),
                        pltpu.VMEM((s, 2 * ATTN_WIDTH), BF16), pltpu.VMEM((2 * tq, ATTN_WIDTH), BF16)],
        compiler_params=_cparams(("arbitrary", "arbitrary")),
        name="diff_attn",
    )(lam, tbl, bucket, qkv, qkv, qkv, subln)


def _proj_out_kernel(x_ref, a_ref, w_ref, gpost_ref, o_ref):
    m = jnp.dot(a_ref[...], w_ref[...], preferred_element_type=F32)
    o_ref[...] = x_ref[...] + _rms(m, gpost_ref[...])


def _proj_out(x2, a2, w, g_post):
    n = x2.shape[0]
    k = a2.shape[1]
    tm = _token_tile(n, 512)
    return pl.pallas_call(
        _proj_out_kernel,
        grid=(n // tm,),
        in_specs=[pl.BlockSpec((tm, D_MODEL), lambda i: (i, 0)), pl.BlockSpec((tm, k), lambda i: (i, 0)),
                  _const_spec((k, D_MODEL)), _const_spec((1, D_MODEL))],
        out_specs=pl.BlockSpec((tm, D_MODEL), lambda i: (i, 0)),
        out_shape=jax.ShapeDtypeStruct((n, D_MODEL), F32),
        compiler_params=_cparams(("parallel",)),
        name="proj_out",
    )(x2, a2, w, g_post)


def _relative_bucket(rel):
    half = N_BUCKETS // 2
    max_exact = half // 2
    ret = jnp.where(rel > 0, half, 0)
    n = jnp.abs(rel)
    nf = jnp.maximum(n, 1).astype(F32)
    large = max_exact + (jnp.log(nf / max_exact) / math.log(MAX_DISTANCE / max_exact) * (half - max_exact)).astype(jnp.int32)
    large = jnp.minimum(large, half - 1)
    return ret + jnp.where(n < max_exact, n, large)


def _bucket_slab(tq):
    a = jnp.arange(tq)[:, None]
    c = jnp.arange((2 * SLAB_REACH + 1) * tq)[None, :]
    return _relative_bucket(c - a - SLAB_REACH * tq).astype(jnp.int32)


def _row(v):
    return v.reshape(1, -1).astype(F32)


def _mamba_layer(x, pre_g, post_g, w_in, conv_w, conv_b, dt_bias, a_log, d_skip, norm_g, w_out):
    b, s, _ = x.shape
    wz = w_in[:, :D_INNER].astype(BF16)
    wx = w_in[:, D_INNER:D_INNER + CONV_DIM].astype(BF16)
    wdt = w_in[:, D_INNER + CONV_DIM:].astype(BF16)
    z, xbc, dt, dtt = _ssm_in(x, _row(pre_g), wz, wx, wdt, wdt.T)
    xs, bm, cm = _conv(xbc, conv_w.astype(F32), _row(conv_b))
    a = -jnp.exp(a_log.astype(F32))
    brow = dt_bias.astype(F32)
    e = jnp.repeat(jnp.eye(SSM_HEADS, dtype=BF16), SSM_HEAD_DIM, axis=1)
    dskip = jnp.repeat(d_skip.astype(F32), SSM_HEAD_DIM).reshape(1, D_INNER)
    common = (xs, bm, cm, dt, dtt, brow, brow[:, :, None], a, a[:, :, None], e)
    y_fwd = _ssd(*common, reverse=False)
    y = _ssd(*common, y_fwd, dskip, reverse=True)
    out = _ssm_out(x.reshape(b * s, D_MODEL), y.reshape(b * s, D_INNER), z.reshape(b * s, D_INNER),
                   _row(norm_g), w_out.astype(BF16), _row(post_g))
    return out.reshape(b, s, D_MODEL)


def _attn_layer(x, pre_g, post_g, w_qkv, lam, subln_g, w_out, rel_bias, lambda_init):
    b, s, _ = x.shape
    scale = ATTN_HEAD_DIM ** -0.5 * LOG2E
    col_scale = jnp.concatenate([jnp.full((ATTN_DIM,), scale, F32), jnp.ones((2 * ATTN_DIM,), F32)])
    w = (w_qkv.astype(F32) * col_scale).astype(BF16)
    qkv = _qkv(x.reshape(b * s, D_MODEL), _row(pre_g), w).reshape(b, s, 3 * ATTN_DIM)
    lf = lam.astype(F32)
    lam_full = jnp.exp(jnp.sum(lf[0] * lf[1])) - jnp.exp(jnp.sum(lf[2] * lf[3])) + lambda_init
    tq = _token_tile(s, ATTN_Q_TILE)
    assert tq >= MAX_DISTANCE, "bias tiles two q tiles away must lie in the saturated buckets"
    tbl = rel_bias.astype(F32).T * LOG2E
    o = _attention(lam_full.reshape(1), tbl, _bucket_slab(tq), qkv, _row(subln_g), out_scale=1.0 - lambda_init)
    out = _proj_out(x.reshape(b * s, D_MODEL), o.reshape(b * s, ATTN_DIM), w_out.astype(BF16), _row(post_g))
    return out.reshape(b, s, D_MODEL)


def _trunk(x, norm_pre, norm_post, ffn_w_gate, ffn_w_up, ffn_w_down, ssm_w_in, ssm_conv_w, ssm_conv_b,
           ssm_dt_bias, ssm_a_log, ssm_d, ssm_norm, ssm_w_out, attn_w_qkv, attn_lambda, attn_subln,
           attn_w_out, rel_bias):
    b, s, _ = x.shape
    depth = norm_pre.shape[0]

    def ffn(x, i, k, slot):
        y = _ffn(x.reshape(b * s, D_MODEL), _row(norm_pre[i, slot]), ffn_w_gate[i, k].astype(BF16),
                 ffn_w_up[i, k].astype(BF16), ffn_w_down[i, k].astype(BF16), _row(norm_post[i, slot]))
        return y.reshape(b, s, D_MODEL)

    for i in range(depth):
        x = ffn(x, i, 0, 0)
        j = i // 2
        if i % 2 == 0:
            x = _mamba_layer(x, norm_pre[i, 1], norm_post[i, 1], ssm_w_in[j], ssm_conv_w[j], ssm_conv_b[j],
                             ssm_dt_bias[j], ssm_a_log[j], ssm_d[j], ssm_norm[j], ssm_w_out[j])
        else:
            lambda_init = 0.8 - 0.6 * math.exp(-0.3 * i)
            x = _attn_layer(x, norm_pre[i, 1], norm_post[i, 1], attn_w_qkv[j], attn_lambda[j], attn_subln[j],
                            attn_w_out[j], rel_bias, lambda_init)
        x = ffn(x, i, 1, 2)
    return x


def kernel(x_prompt, x_sample, norm_pre, norm_post, ffn_w_gate, ffn_w_up, ffn_w_down, ssm_w_in, ssm_conv_w, ssm_conv_b, ssm_dt_bias, ssm_a_log, ssm_d, ssm_norm, ssm_w_out, attn_w_qkv, attn_lambda, attn_subln, attn_w_out, rel_bias):
    params = (norm_pre, norm_post, ffn_w_gate, ffn_w_up, ffn_w_down, ssm_w_in, ssm_conv_w, ssm_conv_b,
              ssm_dt_bias, ssm_a_log, ssm_d, ssm_norm, ssm_w_out, attn_w_qkv, attn_lambda, attn_subln,
              attn_w_out, rel_bias)
    return (_trunk(x_prompt, *params), _trunk(x_sample, *params))
```

```python
import functools
import math

import jax
import jax.numpy as jnp
from jax import lax
from jax.experimental import pallas as pl
from jax.experimental.pallas import tpu as pltpu

F32 = jnp.float32
BF16 = jnp.bfloat16

EPS = 1e-6
D_MODEL = 1024
D_FF = 2816
FF_CHUNK = 256
FFN_TOKEN_TILE = 512

D_INNER = 2048
SSM_HEAD_DIM = 64
SSM_HEADS = 32
SSM_GROUPS = 4
HEADS_PER_GROUP = SSM_HEADS // SSM_GROUPS
D_STATE = 128
D_CONV = 5
CHUNK = 128
SSD_CHUNKS_PER_STEP = 4
BC_DIM = SSM_GROUPS * D_STATE
CONV_DIM = D_INNER + 2 * BC_DIM
GROUP_WIDTH = D_INNER // SSM_GROUPS

ATTN_HEADS = 8
ATTN_HEAD_DIM = 64
ATTN_WIDTH = 2 * ATTN_HEAD_DIM
ATTN_DIM = ATTN_HEADS * ATTN_WIDTH
N_BUCKETS = 32
MAX_DISTANCE = 128
ATTN_Q_TILE = 512
ATTN_Q_UNROLL = 4
ATTN_ROW_BLOCK = 128
SLAB_REACH = 2
LOG2E = math.log2(math.e)

HALO = 16
CONV_ROWS = 128
CONV_WINDOW = 256
VMEM_LIMIT = 56 * 1024 * 1024

NT_DIMS = (((1,), (1,)), ((), ()))
TN_DIMS = (((0,), (0,)), ((), ()))


def _cparams(sem):
    return pltpu.CompilerParams(dimension_semantics=sem, vmem_limit_bytes=VMEM_LIMIT)


def _rms(x, g):
    return x * lax.rsqrt(jnp.mean(x * x, axis=-1, keepdims=True) + EPS) * g


def _silu(x):
    return x * (1.0 / (1.0 + jnp.exp(-x)))


def _const_spec(shape):
    nd = len(shape)
    return pl.BlockSpec(shape, lambda *_: (0,) * nd, pipeline_mode=pl.Buffered(1))


def _token_tile(n, pref):
    t = min(pref, n)
    while n % t:
        t //= 2
    return t


def _ffn_kernel(x_ref, gpre_ref, wg_ref, wu_ref, wd_ref, gpost_ref, o_ref, xn_ref, acc_ref):
    xn_ref[...] = _rms(x_ref[...], gpre_ref[...]).astype(BF16)
    for c in range(D_FF // FF_CHUNK):
        sl = slice(c * FF_CHUNK, (c + 1) * FF_CHUNK)
        xn = xn_ref[...]
        g = jnp.dot(xn, wg_ref[:, sl], preferred_element_type=F32)
        u = jnp.dot(xn, wu_ref[:, sl], preferred_element_type=F32)
        h = (_silu(g) * u).astype(BF16)
        d = jnp.dot(h, wd_ref[sl, :], preferred_element_type=F32)
        if c == 0:
            acc_ref[...] = d
        else:
            acc_ref[...] += d
    o_ref[...] = x_ref[...] + 0.5 * _rms(acc_ref[...], gpost_ref[...])


def _mix_ffn_kernel(x_ref, a_ref, wo_ref, gmix_ref, gpre_ref, wg_ref, wu_ref, wd_ref, gpost_ref, o_ref, xn_ref,
                    acc_ref, xres_ref):
    m = jnp.dot(a_ref[...], wo_ref[...], preferred_element_type=F32)
    xres_ref[...] = x_ref[...] + _rms(m, gmix_ref[...])
    _ffn_kernel(xres_ref, gpre_ref, wg_ref, wu_ref, wd_ref, gpost_ref, o_ref, xn_ref, acc_ref)


def _mix_ffn(x2, a2, wo, g_mix, g_pre, wg, wu, wd, g_post):
    n = x2.shape[0]
    k = a2.shape[1]
    tm = _token_tile(n, FFN_TOKEN_TILE)
    row = pl.BlockSpec((tm, D_MODEL), lambda i: (i, 0))
    return pl.pallas_call(
        _mix_ffn_kernel,
        grid=(n // tm,),
        in_specs=[row, pl.BlockSpec((tm, k), lambda i: (i, 0)), _const_spec((k, D_MODEL)), _const_spec((1, D_MODEL)),
                  _const_spec((1, D_MODEL)), _const_spec((D_MODEL, D_FF)), _const_spec((D_MODEL, D_FF)),
                  _const_spec((D_FF, D_MODEL)), _const_spec((1, D_MODEL))],
        out_specs=row,
        out_shape=jax.ShapeDtypeStruct((n, D_MODEL), F32),
        scratch_shapes=[pltpu.VMEM((tm, D_MODEL), BF16), pltpu.VMEM((tm, D_MODEL), F32),
                        pltpu.VMEM((tm, D_MODEL), F32)],
        compiler_params=_cparams(("parallel",)),
        name="mix_ffn",
    )(x2, a2, wo, g_mix, g_pre, wg, wu, wd, g_post)


def _ffn(x2, g_pre, wg, wu, wd, g_post):
    n = x2.shape[0]
    tm = _token_tile(n, FFN_TOKEN_TILE)
    row = pl.BlockSpec((tm, D_MODEL), lambda i: (i, 0))
    return pl.pallas_call(
        _ffn_kernel,
        grid=(n // tm,),
        in_specs=[row, _const_spec((1, D_MODEL)), _const_spec((D_MODEL, D_FF)), _const_spec((D_MODEL, D_FF)),
                  _const_spec((D_FF, D_MODEL)), _const_spec((1, D_MODEL))],
        out_specs=row,
        out_shape=jax.ShapeDtypeStruct((n, D_MODEL), F32),
        scratch_shapes=[pltpu.VMEM((tm, D_MODEL), BF16), pltpu.VMEM((tm, D_MODEL), F32)],
        compiler_params=_cparams(("parallel",)),
        name="ffn",
    )(x2, g_pre, wg, wu, wd, g_post)


def _ssm_in_kernel(x_ref, g_ref, wz_ref, wx_ref, wdt_ref, wdtt_ref, z_ref, xbc_ref, dt_ref, dtt_ref):
    u = _rms(x_ref[0], g_ref[...]).astype(BF16)
    z_ref[0] = jnp.dot(u, wz_ref[...], preferred_element_type=F32).astype(BF16)
    xbc_ref[0] = jnp.dot(u, wx_ref[...], preferred_element_type=F32).astype(BF16)
    dt_ref[0] = jnp.dot(u, wdt_ref[...], preferred_element_type=F32)
    dtt_ref[0] = lax.dot_general(wdtt_ref[...], u, NT_DIMS, preferred_element_type=F32)


def _ssm_in(x, g, wz, wx, wdt, wdtt):
    b, s, _ = x.shape
    tm = _token_tile(s, 512)
    ndt = 2 * SSM_HEADS

    def tok(width):
        return pl.BlockSpec((1, tm, width), lambda i, j: (i, j, 0))

    return pl.pallas_call(
        _ssm_in_kernel,
        grid=(b, s // tm),
        in_specs=[tok(D_MODEL), _const_spec((1, D_MODEL)), _const_spec((D_MODEL, D_INNER)),
                  _const_spec((D_MODEL, CONV_DIM)), _const_spec((D_MODEL, ndt)), _const_spec((ndt, D_MODEL))],
        out_specs=[tok(D_INNER), tok(CONV_DIM), tok(ndt), pl.BlockSpec((1, ndt, tm), lambda i, j: (i, 0, j))],
        out_shape=[jax.ShapeDtypeStruct((b, s, D_INNER), BF16), jax.ShapeDtypeStruct((b, s, CONV_DIM), BF16),
                   jax.ShapeDtypeStruct((b, s, ndt), F32), jax.ShapeDtypeStruct((b, ndt, s), F32)],
        compiler_params=_cparams(("parallel", "parallel")),
        name="ssm_in",
    )(x, g, wz, wx, wdt, wdtt)


def _conv_kernel(prev_ref, main_ref, next_ref, w_ref, b_ref, xs_ref, bm_ref, cm_ref, ext_ref):
    j = pl.program_id(1)
    tc = main_ref.shape[1]
    zero = jnp.zeros((HALO, CONV_DIM), BF16)
    ext_ref[pl.ds(0, HALO), :] = jnp.where(j > 0, prev_ref[0], zero)
    ext_ref[pl.ds(HALO, tc), :] = main_ref[0]
    ext_ref[pl.ds(HALO + tc, HALO), :] = jnp.where(j < pl.num_programs(1) - 1, next_ref[0], zero)
    pad = D_CONV // 2
    rows = min(CONV_ROWS, tc)
    win_len = min(CONV_WINDOW, tc + 2 * HALO)
    taps = [k for k in range(D_CONV) if k != pad]
    for r0 in range(0, tc, rows):
        w0 = min(r0, tc + 2 * HALO - win_len)
        out_row = lax.broadcasted_iota(jnp.int32, (rows, win_len), 0) + (HALO + r0 - w0 - pad)
        win_row = lax.broadcasted_iota(jnp.int32, (rows, win_len), 1)
        sel = jnp.concatenate([jnp.where(win_row == out_row + k, 1.0, 0.0).astype(BF16) for k in taps], axis=0)
        shifted = jnp.dot(sel, ext_ref[pl.ds(w0, win_len), :], preferred_element_type=F32)
        acc = b_ref[...] + w_ref[pad:pad + 1, :] * ext_ref[pl.ds(HALO + r0, rows), :].astype(F32)
        for n, k in enumerate(taps):
            acc = acc + w_ref[k:k + 1, :] * shifted[n * rows:(n + 1) * rows]
        y = _silu(acc).astype(BF16)
        xs_ref[0, pl.ds(r0, rows), :] = y[:, :D_INNER]
        bm_ref[0, pl.ds(r0, rows), :] = y[:, D_INNER:D_INNER + BC_DIM]
        cm_ref[0, pl.ds(r0, rows), :] = y[:, D_INNER + BC_DIM:]


def _conv(xbc, w, bias):
    b, s, _ = xbc.shape
    tc = _token_tile(s, 256)
    nb = tc // HALO
    last = s // HALO - 1

    def tok(width):
        return pl.BlockSpec((1, tc, width), lambda i, j: (i, j, 0))

    return pl.pallas_call(
        _conv_kernel,
        grid=(b, s // tc),
        in_specs=[pl.BlockSpec((1, HALO, CONV_DIM), lambda i, j: (i, jnp.maximum(j * nb - 1, 0), 0)),
                  tok(CONV_DIM),
                  pl.BlockSpec((1, HALO, CONV_DIM), lambda i, j: (i, jnp.minimum((j + 1) * nb, last), 0)),
                  _const_spec((D_CONV, CONV_DIM)), _const_spec((1, CONV_DIM))],
        out_specs=[tok(D_INNER), tok(BC_DIM), tok(BC_DIM)],
        out_shape=[jax.ShapeDtypeStruct((b, s, D_INNER), BF16), jax.ShapeDtypeStruct((b, s, BC_DIM), BF16),
                   jax.ShapeDtypeStruct((b, s, BC_DIM), BF16)],
        scratch_shapes=[pltpu.VMEM((tc + 2 * HALO, CONV_DIM), BF16)],
        compiler_params=_cparams(("parallel", "parallel")),
        name="ssm_conv",
    )(xbc, xbc, xbc, w, bias)


def _split3(v):
    hi = v.astype(BF16)
    r1 = v - hi.astype(F32)
    mid = r1.astype(BF16)
    lo = (r1 - mid.astype(F32)).astype(BF16)
    return hi, mid, lo


def _cumsum_rows(x, lower):
    return sum(jnp.dot(lower, p, preferred_element_type=F32) for p in _split3(x))


def _cumsum_lanes(x, upper):
    return sum(jnp.dot(p, upper, preferred_element_type=F32) for p in _split3(x))


def _expand_heads(v, e):
    hi = v.astype(BF16)
    lo = (v - hi.astype(F32)).astype(BF16)
    return jnp.dot(hi, e, preferred_element_type=F32) + jnp.dot(lo, e, preferred_element_type=F32)


def _ssd_kernel(*refs, reverse, final):
    xs_ref, st_ref = refs[0], refs[-1]

    @pl.when(pl.program_id(1) == 0)
    def _():
        st_ref[...] = jnp.zeros_like(st_ref)

    n_sub = xs_ref.shape[1] // CHUNK
    for sub in (range(n_sub - 1, -1, -1) if reverse else range(n_sub)):
        _ssd_chunk(refs, slice(sub * CHUNK, (sub + 1) * CHUNK), reverse=reverse, final=final)


def _ssd_chunk(refs, rs, *, reverse, final):
    if final:
        (xs_ref, bm_ref, cm_ref, dt_ref, dtt_ref, brow_ref, bcol_ref, arow_ref, acol_ref, e_ref,
         prev_ref, dskip_ref, y_ref, st_ref) = refs
    else:
        (xs_ref, bm_ref, cm_ref, dt_ref, dtt_ref, brow_ref, bcol_ref, arow_ref, acol_ref, e_ref,
         y_ref, st_ref) = refs
    d = 1 if reverse else 0
    hs = slice(d * SSM_HEADS, (d + 1) * SSM_HEADS)
    dt_c = jax.nn.softplus(dt_ref[0, rs, hs] + brow_ref[d:d + 1, :])
    dt_r = jax.nn.softplus(dtt_ref[0, hs, rs] + bcol_ref[d])
    da_c = dt_c * arow_ref[d:d + 1, :]
    da_r = dt_r * acol_ref[d]
    ii = lax.broadcasted_iota(jnp.int32, (CHUNK, CHUNK), 0)
    jj = lax.broadcasted_iota(jnp.int32, (CHUNK, CHUNK), 1)
    cs_c = _cumsum_rows(da_c, jnp.where(jj <= ii, 1.0, 0.0).astype(BF16))
    cs_r = _cumsum_lanes(da_r, jnp.where(ii <= jj, 1.0, 0.0).astype(BF16))
    tot = cs_c[CHUNK - 1:CHUNK, :]
    if reverse:
        a_c = da_c - cs_c
        a_r = da_r - cs_r
        off_scale = jnp.exp(tot + a_c)
        st_w = jnp.exp(-a_c) * dt_c
    else:
        a_c = cs_c
        a_r = cs_r
        off_scale = jnp.exp(cs_c)
        st_w = jnp.exp(tot - cs_c) * dt_c
    e = e_ref[...]
    off_x = _expand_heads(off_scale, e)
    stw_x = _expand_heads(st_w, e)
    dec_x = _expand_heads(jnp.broadcast_to(jnp.exp(tot), (HALO, SSM_HEADS)), e)[0:1, :]

    keep = (jj >= ii) if reverse else (ii >= jj)
    first_head = jj < SSM_HEAD_DIM

    for g in range(SSM_GROUPS):
        gs = slice(g * D_STATE, (g + 1) * D_STATE)
        ws = slice(g * GROUP_WIDTH, (g + 1) * GROUP_WIDTH)
        bg = bm_ref[0, rs, gs]
        cg = cm_ref[0, rs, gs]
        cb = lax.dot_general(cg, bg, NT_DIMS, preferred_element_type=F32)
        st = st_ref[:, ws]
        y_off = jnp.dot(cg, st.astype(BF16), preferred_element_type=F32) * off_x[:, ws]
        xw = (xs_ref[0, rs, ws].astype(F32) * stw_x[:, ws]).astype(BF16)
        st_ref[:, ws] = st * dec_x[:, ws] + lax.dot_general(bg, xw, TN_DIMS, preferred_element_type=F32)
        for pr in range(HEADS_PER_GROUP // 2):
            h0 = g * HEADS_PER_GROUP + 2 * pr
            ps = slice(h0 * SSM_HEAD_DIM, (h0 + 2) * SSM_HEAD_DIM)
            x_p = xs_ref[0, rs, ps]
            ys = []
            for h in (h0, h0 + 1):
                seg = a_c[:, h:h + 1] - a_r[h:h + 1, :]
                w = cb * jnp.exp(jnp.where(keep, seg, -jnp.inf)) * dt_r[h:h + 1, :]
                ys.append(jnp.dot(w.astype(BF16), x_p, preferred_element_type=F32))
            y_p = jnp.where(first_head, ys[0], ys[1]) + y_off[:, pr * 2 * SSM_HEAD_DIM:(pr + 1) * 2 * SSM_HEAD_DIM]
            if final:
                y_p = y_p + prev_ref[0, rs, ps].astype(F32) + x_p.astype(F32) * dskip_ref[:, ps]
            y_ref[0, rs, ps] = y_p.astype(BF16)


def _ssd(xs, bm, cm, dt, dtt, brow, bcol, arow, acol, e, prev=None, dskip=None, *, reverse):
    b, s, _ = xs.shape
    rows = _token_tile(s, SSD_CHUNKS_PER_STEP * CHUNK)
    nc = s // rows
    final = prev is not None
    ndt = 2 * SSM_HEADS

    def cidx(j):
        return nc - 1 - j if reverse else j

    def tok(width):
        return pl.BlockSpec((1, rows, width), lambda i, j: (i, cidx(j), 0))

    in_specs = [tok(D_INNER), tok(BC_DIM), tok(BC_DIM), tok(ndt),
                pl.BlockSpec((1, ndt, rows), lambda i, j: (i, 0, cidx(j))),
                _const_spec((2, SSM_HEADS)), _const_spec((2, SSM_HEADS, 1)),
                _const_spec((2, SSM_HEADS)), _const_spec((2, SSM_HEADS, 1)),
                _const_spec((SSM_HEADS, D_INNER))]
    args = [xs, bm, cm, dt, dtt, brow, bcol, arow, acol, e]
    if final:
        in_specs += [tok(D_INNER), _const_spec((1, D_INNER))]
        args += [prev, dskip]
    return pl.pallas_call(
        functools.partial(_ssd_kernel, reverse=reverse, final=final),
        grid=(b, nc),
        in_specs=in_specs,
        out_specs=tok(D_INNER),
        out_shape=jax.ShapeDtypeStruct((b, s, D_INNER), BF16),
        scratch_shapes=[pltpu.VMEM((D_STATE, D_INNER), F32)],
        compiler_params=_cparams(("parallel", "arbitrary")),
        name="ssd_bwd" if reverse else "ssd_fwd",
    )(*args)


def _ssm_out_kernel(x_ref, y_ref, z_ref, ng_ref, w_ref, gpost_ref, o_ref):
    y = y_ref[...].astype(F32) * _silu(z_ref[...].astype(F32))
    parts = []
    for g in range(SSM_GROUPS):
        yg = y[:, g * GROUP_WIDTH:(g + 1) * GROUP_WIDTH]
        parts.append(yg * lax.rsqrt(jnp.mean(yg * yg, axis=-1, keepdims=True) + EPS))
    yn = (jnp.concatenate(parts, axis=-1) * ng_ref[...]).astype(BF16)
    m = jnp.dot(yn, w_ref[...], preferred_element_type=F32)
    o_ref[...] = x_ref[...] + _rms(m, gpost_ref[...])


def _ssm_out(x2, y2, z2, ng, w, g_post):
    n = x2.shape[0]
    tm = _token_tile(n, 512)

    def row(width):
        return pl.BlockSpec((tm, width), lambda i: (i, 0))

    return pl.pallas_call(
        _ssm_out_kernel,
        grid=(n // tm,),
        in_specs=[row(D_MODEL), row(D_INNER), row(D_INNER), _const_spec((1, D_INNER)),
                  _const_spec((D_INNER, D_MODEL)), _const_spec((1, D_MODEL))],
        out_specs=row(D_MODEL),
        out_shape=jax.ShapeDtypeStruct((n, D_MODEL), F32),
        compiler_params=_cparams(("parallel",)),
        name="ssm_out",
    )(x2, y2, z2, ng, w, g_post)


def _qkv_kernel(x_ref, g_ref, w_ref, o_ref):
    u = _rms(x_ref[...], g_ref[...]).astype(BF16)
    o_ref[...] = jnp.dot(u, w_ref[...], preferred_element_type=F32).astype(BF16)


def _qkv(x2, g, w):
    n = x2.shape[0]
    tm = _token_tile(n, 512)
    return pl.pallas_call(
        _qkv_kernel,
        grid=(n // tm,),
        in_specs=[pl.BlockSpec((tm, D_MODEL), lambda i: (i, 0)), _const_spec((1, D_MODEL)),
                  _const_spec((D_MODEL, 3 * ATTN_DIM))],
        out_specs=pl.BlockSpec((tm, 3 * ATTN_DIM), lambda i: (i, 0)),
        out_shape=jax.ShapeDtypeStruct((n, 3 * ATTN_DIM), BF16),
        compiler_params=_cparams(("parallel",)),
        name="attn_qkv",
    )(x2, g, w)


def _attn_kernel(lam_ref, tbl_ref, bucket_ref, q_ref, k_ref, v_ref, g_ref, o_ref, slab_ref, s_ref, p_ref, vv_ref,
                 q2_ref, *, out_scale):
    tq = slab_ref.shape[0]
    s = k_ref.shape[1]
    nk = s // tq
    h = pl.program_id(0)

    @pl.when(pl.program_id(1) == 0)
    def _():
        bk = bucket_ref[...]
        acc = jnp.zeros(bk.shape, F32)
        for kk in range(N_BUCKETS):
            acc = jnp.where(bk == kk, tbl_ref[h, kk], acc)
        slab_ref[...] = acc

    lam = lam_ref[0]
    g = g_ref[...]
    vv_ref[...] = jnp.concatenate([v_ref[0], jnp.ones((s, ATTN_WIDTH), BF16)], axis=1)
    kchunk = 2 * tq if s % (2 * tq) == 0 else tq
    rblk = min(ATTN_ROW_BLOCK, tq)

    def q_tile(qi, carry):
        q0 = pl.multiple_of(qi * tq, tq)
        q = q_ref[0, pl.ds(q0, tq), :]
        lane = lax.broadcasted_iota(jnp.int32, q.shape, 1)
        zero = jnp.zeros_like(q)
        q2_ref[:tq, :] = jnp.where(lane < ATTN_HEAD_DIM, q, zero)
        q2_ref[tq:, :] = jnp.where(lane >= ATTN_HEAD_DIM, q, zero)
        mxs = []
        for r0 in range(0, 2 * tq, rblk):
            a0 = r0 % tq
            mrun = jnp.full((rblk, 128), -jnp.inf, F32)
            for k0 in range(0, s, kchunk):
                tiles = []
                for kc in range(k0 // tq, (k0 + kchunk) // tq):
                    d = jnp.clip(kc - qi, -SLAB_REACH, SLAB_REACH) + SLAB_REACH
                    tiles.append(slab_ref[a0:a0 + rblk, pl.ds(pl.multiple_of(d * tq, tq), tq)])
                bias = jnp.concatenate(tiles, axis=1) if len(tiles) > 1 else tiles[0]
                sc = lax.dot_general(q2_ref[r0:r0 + rblk, :], k_ref[0, k0:k0 + kchunk, :], NT_DIMS,
                                     preferred_element_type=F32) + bias
                s_ref[r0:r0 + rblk, k0:k0 + kchunk] = sc
                for j in range(kchunk // 128):
                    mrun = jnp.maximum(mrun, sc[:, j * 128:(j + 1) * 128])
            mxs.append(jnp.max(mrun, axis=-1, keepdims=True))
        mx = jnp.concatenate(mxs, axis=0)
        p_ref[...] = jnp.exp2(s_ref[...] - mx).astype(BF16)
        vv = vv_ref[...]
        acc1 = jnp.dot(p_ref[:tq, :], vv, preferred_element_type=F32)
        acc2 = jnp.dot(p_ref[tq:, :], vv, preferred_element_type=F32)
        o1 = acc1[:, :ATTN_WIDTH] / acc1[:, ATTN_WIDTH:ATTN_WIDTH + 1]
        o2 = acc2[:, :ATTN_WIDTH] / acc2[:, ATTN_WIDTH:ATTN_WIDTH + 1]
        o_ref[0, pl.ds(q0, tq), :] = (_rms(o1 - lam * o2, g) * out_scale).astype(BF16)
        return carry

    nq = s // tq
    lax.fori_loop(0, nq, q_tile, 0, unroll=next(u for u in (ATTN_Q_UNROLL, 2, 1) if nq % u == 0))


def _attention(lam, tbl, bucket, qkv, subln, *, out_scale):
    b, s, _ = qkv.shape
    tq = bucket.shape[0]

    def head_block(col0):
        return pl.BlockSpec((1, s, ATTN_WIDTH), lambda h, i: (i, 0, col0 + h))

    return pl.pallas_call(
        functools.partial(_attn_kernel, out_scale=out_scale),
        grid=(ATTN_HEADS, b),
        in_specs=[pl.BlockSpec(memory_space=pltpu.SMEM), pl.BlockSpec(memory_space=pltpu.SMEM),
                  _const_spec(bucket.shape), head_block(0), head_block(ATTN_HEADS), head_block(2 * ATTN_HEADS),
                  _const_spec((1, ATTN_WIDTH))],
        out_specs=head_block(0),
        out_shape=jax.ShapeDtypeStruct((b, s, ATTN_DIM), BF16),
        scratch_shapes=[pltpu.VMEM(bucket.shape, F32), pltpu.VMEM((2 * tq, s), F32), pltpu.VMEM((2 * tq, s), BF16),
                        pltpu.VMEM((s, 2 * ATTN_WIDTH), BF16), pltpu.VMEM((2 * tq, ATTN_WIDTH), BF16)],
        compiler_params=_cparams(("arbitrary", "arbitrary")),
        name="diff_attn",
    )(lam, tbl, bucket, qkv, qkv, qkv, subln)


def _proj_out_kernel(x_ref, a_ref, w_ref, gpost_ref, o_ref):
    m = jnp.dot(a_ref[...], w_ref[...], preferred_element_type=F32)
    o_ref[...] = x_ref[...] + _rms(m, gpost_ref[...])


def _proj_out(x2, a2, w, g_post):
    n = x2.shape[0]
    k = a2.shape[1]
    tm = _token_tile(n, 512)
    return pl.pallas_call(
        _proj_out_kernel,
        grid=(n // tm,),
        in_specs=[pl.BlockSpec((tm, D_MODEL), lambda i: (i, 0)), pl.BlockSpec((tm, k), lambda i: (i, 0)),
                  _const_spec((k, D_MODEL)), _const_spec((1, D_MODEL))],
        out_specs=pl.BlockSpec((tm, D_MODEL), lambda i: (i, 0)),
        out_shape=jax.ShapeDtypeStruct((n, D_MODEL), F32),
        compiler_params=_cparams(("parallel",)),
        name="proj_out",
    )(x2, a2, w, g_post)


def _relative_bucket(rel):
    half = N_BUCKETS // 2
    max_exact = half // 2
    ret = jnp.where(rel > 0, half, 0)
    n = jnp.abs(rel)
    nf = jnp.maximum(n, 1).astype(F32)
    large = max_exact + (jnp.log(nf / max_exact) / math.log(MAX_DISTANCE / max_exact) * (half - max_exact)).astype(jnp.int32)
    large = jnp.minimum(large, half - 1)
    return ret + jnp.where(n < max_exact, n, large)


def _bucket_slab(tq):
    a = jnp.arange(tq)[:, None]
    c = jnp.arange((2 * SLAB_REACH + 1) * tq)[None, :]
    return _relative_bucket(c - a - SLAB_REACH * tq).astype(jnp.int32)


def _row(v):
    return v.reshape(1, -1).astype(F32)


def _mamba_layer(x, pre_g, post_g, w_in, conv_w, conv_b, dt_bias, a_log, d_skip, norm_g, w_out):
    b, s, _ = x.shape
    wz = w_in[:, :D_INNER].astype(BF16)
    wx = w_in[:, D_INNER:D_INNER + CONV_DIM].astype(BF16)
    wdt = w_in[:, D_INNER + CONV_DIM:].astype(BF16)
    z, xbc, dt, dtt = _ssm_in(x, _row(pre_g), wz, wx, wdt, wdt.T)
    xs, bm, cm = _conv(xbc, conv_w.astype(F32), _row(conv_b))
    a = -jnp.exp(a_log.astype(F32))
    brow = dt_bias.astype(F32)
    e = jnp.repeat(jnp.eye(SSM_HEADS, dtype=BF16), SSM_HEAD_DIM, axis=1)
    dskip = jnp.repeat(d_skip.astype(F32), SSM_HEAD_DIM).reshape(1, D_INNER)
    common = (xs, bm, cm, dt, dtt, brow, brow[:, :, None], a, a[:, :, None], e)
    y_fwd = _ssd(*common, reverse=False)
    y = _ssd(*common, y_fwd, dskip, reverse=True)
    out = _ssm_out(x.reshape(b * s, D_MODEL), y.reshape(b * s, D_INNER), z.reshape(b * s, D_INNER),
                   _row(norm_g), w_out.astype(BF16), _row(post_g))
    return out.reshape(b, s, D_MODEL)


def _attn_layer(x, pre_g, post_g, w_qkv, lam, subln_g, w_out, rel_bias, lambda_init):
    b, s, _ = x.shape
    scale = ATTN_HEAD_DIM ** -0.5 * LOG2E
    col_scale = jnp.concatenate([jnp.full((ATTN_DIM,), scale, F32), jnp.ones((2 * ATTN_DIM,), F32)])
    w = (w_qkv.astype(F32) * col_scale).astype(BF16)
    qkv = _qkv(x.reshape(b * s, D_MODEL), _row(pre_g), w).reshape(b, s, 3 * ATTN_DIM)
    lf = lam.astype(F32)
    lam_full = jnp.exp(jnp.sum(lf[0] * lf[1])) - jnp.exp(jnp.sum(lf[2] * lf[3])) + lambda_init
    tq = _token_tile(s, ATTN_Q_TILE)
    assert tq >= MAX_DISTANCE, "bias tiles two q tiles away must lie in the saturated buckets"
    tbl = rel_bias.astype(F32).T * LOG2E
    o = _attention(lam_full.reshape(1), tbl, _bucket_slab(tq), qkv, _row(subln_g), out_scale=1.0 - lambda_init)
    return o.reshape(b * s, ATTN_DIM), w_out.astype(BF16), _row(post_g)


def _trunk(x, norm_pre, norm_post, ffn_w_gate, ffn_w_up, ffn_w_down, ssm_w_in, ssm_conv_w, ssm_conv_b,
           ssm_dt_bias, ssm_a_log, ssm_d, ssm_norm, ssm_w_out, attn_w_qkv, attn_lambda, attn_subln,
           attn_w_out, rel_bias):
    b, s, _ = x.shape
    depth = norm_pre.shape[0]

    def ffn(x, i, k, slot):
        y = _ffn(x.reshape(b * s, D_MODEL), _row(norm_pre[i, slot]), ffn_w_gate[i, k].astype(BF16),
                 ffn_w_up[i, k].astype(BF16), ffn_w_down[i, k].astype(BF16), _row(norm_post[i, slot]))
        return y.reshape(b, s, D_MODEL)

    for i in range(depth):
        x = ffn(x, i, 0, 0)
        j = i // 2
        if i % 2 == 0:
            x = _mamba_layer(x, norm_pre[i, 1], norm_post[i, 1], ssm_w_in[j], ssm_conv_w[j], ssm_conv_b[j],
                             ssm_dt_bias[j], ssm_a_log[j], ssm_d[j], ssm_norm[j], ssm_w_out[j])
            x = ffn(x, i, 1, 2)
        else:
            lambda_init = 0.8 - 0.6 * math.exp(-0.3 * i)
            a2, wo, g_mix = _attn_layer(x, norm_pre[i, 1], norm_post[i, 1], attn_w_qkv[j], attn_lambda[j],
                                        attn_subln[j], attn_w_out[j], rel_bias, lambda_init)
            x = _mix_ffn(x.reshape(b * s, D_MODEL), a2, wo, g_mix, _row(norm_pre[i, 2]),
                         ffn_w_gate[i, 1].astype(BF16), ffn_w_up[i, 1].astype(BF16), ffn_w_down[i, 1].astype(BF16),
                         _row(norm_post[i, 2])).reshape(b, s, D_MODEL)
    return x


def kernel(x_prompt, x_sample, norm_pre, norm_post, ffn_w_gate, ffn_w_up, ffn_w_down, ssm_w_in, ssm_conv_w, ssm_conv_b, ssm_dt_bias, ssm_a_log, ssm_d, ssm_norm, ssm_w_out, attn_w_qkv, attn_lambda, attn_subln, attn_w_out, rel_bias):
    params = (norm_pre, norm_post, ffn_w_gate, ffn_w_up, ffn_w_down, ssm_w_in, ssm_conv_w, ssm_conv_b,
              ssm_dt_bias, ssm_a_log, ssm_d, ssm_norm, ssm_w_out, attn_w_qkv, attn_lambda, attn_subln,
              attn_w_out, rel_bias)
    return (_trunk(x_prompt, *params), _trunk(x_sample, *params))
```
